```python
import math
import jax, jax.numpy as jnp
from jax import lax
import numpy as np

D_MODEL = 2048
BATCH = 4
SEQ = 4096
DEPTH = 4

CHUNK = 64

D_MIX = 2 * D_MODEL
POOL_W = D_MIX // 2
SSD_W = D_MIX - POOL_W

POOL_WINDOWS = (2, 4, 8, 16)
POOL_GROUPS = len(POOL_WINDOWS)
POOL_GROUP_W = POOL_W // POOL_GROUPS
MAX_WINDOW = max(POOL_WINDOWS)

SSD_HEAD_DIM = 64
SSD_HEADS = SSD_W // SSD_HEAD_DIM
SSD_STATE = 128
SSD_GROUPS = 4
SSD_HEADS_PER_GROUP = SSD_HEADS // SSD_GROUPS
CONV_WIDTH = 4
CONV_DIM = SSD_W + 2 * SSD_GROUPS * SSD_STATE

D_IN_PROJ = 2 * POOL_W + 2 * SSD_W + 2 * SSD_GROUPS * SSD_STATE + SSD_HEADS

NORM_EPS = 1e-6

kernel_name = "hybrid_pool_ssd_sandwich_trunk"


def rms_norm(x, w):
    xf = x.astype(jnp.float32)
    var = jnp.mean(xf * xf, axis=-1, keepdims=True)
    return (xf * lax.rsqrt(var + NORM_EPS) * w.astype(jnp.float32)).astype(x.dtype)


def pool_mixer(u, mix_w, scale):
    b, l, _ = u.shape
    uf = u.astype(jnp.float32)
    cs = jnp.cumsum(uf, axis=1)
    csp = jnp.pad(cs, ((0, 0), (MAX_WINDOW, 0), (0, 0)))
    t = jnp.arange(l)
    outs = []
    for g, w in enumerate(POOL_WINDOWS):
        sl = slice(g * POOL_GROUP_W, (g + 1) * POOL_GROUP_W)
        win_sum = csp[:, MAX_WINDOW:, sl] - csp[:, MAX_WINDOW - w:MAX_WINDOW - w + l, sl]
        cnt = jnp.minimum(t + 1, w).astype(jnp.float32)[None, :, None]
        outs.append(win_sum / cnt - uf[:, :, sl])
    pooled = jnp.stack(outs, axis=2).astype(u.dtype)
    mixed = jnp.einsum('blgc,gcd->blgd', pooled, mix_w)
    return mixed.reshape(b, l, POOL_W) * scale


def causal_dwconv(u, w, bias):
    y = lax.conv_general_dilated(
        u, w[:, None, :], window_strides=(1,), padding=[(CONV_WIDTH - 1, 0)],
        dimension_numbers=('NWC', 'WIO', 'NWC'), feature_group_count=u.shape[-1])
    return y + bias


def ssd_scan(xh, dt, a, bm, cm):
    b, l, h, p = xh.shape
    c = l // CHUNK
    G, R, N, Q = SSD_GROUPS, SSD_HEADS_PER_GROUP, SSD_STATE, CHUNK
    xdt = (xh * dt[..., None]).reshape(b, c, Q, G, R, p)
    adt = (dt * a).reshape(b, c, Q, G, R).transpose(0, 3, 4, 1, 2)
    acs = jnp.cumsum(adt, axis=-1)
    bm = bm.reshape(b, c, Q, G, N)
    cm = cm.reshape(b, c, Q, G, N)
    mask = jnp.tril(jnp.ones((Q, Q), dtype=bool))
    seg = acs[..., :, None] - acs[..., None, :]
    decay = jnp.exp(jnp.where(mask, seg, -jnp.inf))
    scores = jnp.einsum('bclgn,bcsgn->bgcls', cm, bm)
    y_diag = jnp.einsum('bgcls,bgrcls,bcsgrp->bclgrp', scores, decay, xdt)
    decay_states = jnp.exp(acs[..., -1:] - acs)
    states = jnp.einsum('bcsgn,bgrcs,bcsgrp->bcgrpn', bm, decay_states, xdt)
    chunk_decay = jnp.exp(acs[..., -1])

    def step(carry, inp):
        st, dec = inp
        return carry * dec[..., None, None] + st, carry

    init = jnp.zeros((b, G, R, p, N), dtype=jnp.float32)
    _, prev = lax.scan(step, init, (jnp.moveaxis(states, 1, 0), jnp.moveaxis(chunk_decay, 3, 0)))
    prev = jnp.moveaxis(prev, 0, 1)
    y_off = jnp.einsum('bclgn,bcgrpn,bgrcl->bclgrp', cm, prev, jnp.exp(acs))
    return (y_diag + y_off).reshape(b, l, h, p)


def ssd_branch(z, xbc_raw, dt_raw, conv_w, conv_b, dt_bias, a_log, d_skip, norm_w):
    b, l, _ = z.shape
    xbc = jax.nn.silu(causal_dwconv(xbc_raw, conv_w, conv_b))
    xs = xbc[..., :SSD_W]
    bm = xbc[..., SSD_W:SSD_W + SSD_GROUPS * SSD_STATE].reshape(b, l, SSD_GROUPS, SSD_STATE)
    cm = xbc[..., SSD_W + SSD_GROUPS * SSD_STATE:].reshape(b, l, SSD_GROUPS, SSD_STATE)
    dt = jax.nn.softplus(dt_raw.astype(jnp.float32) + dt_bias.astype(jnp.float32))
    a = -jnp.exp(a_log.astype(jnp.float32))
    xh = xs.reshape(b, l, SSD_HEADS, SSD_HEAD_DIM).astype(jnp.float32)
    y = ssd_scan(xh, dt, a, bm.astype(jnp.float32), cm.astype(jnp.float32))
    y = y + d_skip.astype(jnp.float32)[:, None] * xh
    y = y.reshape(b, l, SSD_W) * jax.nn.silu(z.astype(jnp.float32))
    yg = y.reshape(b, l, SSD_GROUPS, SSD_W // SSD_GROUPS)
    yg = yg * lax.rsqrt(jnp.mean(yg * yg, axis=-1, keepdims=True) + NORM_EPS)
    y = yg.reshape(b, l, SSD_W) * norm_w.astype(jnp.float32)
    return y.astype(z.dtype)


def setup_inputs(seed: int = 0) -> dict:
    key = jax.random.key(seed)
    ks = jax.random.split(key, 16)
    f32 = jnp.float32
    x = jax.random.normal(ks[0], (BATCH, SEQ, D_MODEL), f32)
    pre_norm_w = 1.0 + 0.02 * jax.random.normal(ks[1], (DEPTH, D_MODEL), f32)
    w_in = jax.random.normal(ks[2], (DEPTH, D_MODEL, D_IN_PROJ), f32) * D_MODEL ** -0.5
    pool_mix_w = jax.random.normal(ks[3], (DEPTH, POOL_GROUPS, POOL_GROUP_W, POOL_GROUP_W), f32) * POOL_GROUP_W ** -0.5
    pool_scale = 1.0 + 0.02 * jax.random.normal(ks[4], (DEPTH, POOL_W), f32)
    conv_w = jax.random.normal(ks[5], (DEPTH, CONV_WIDTH, CONV_DIM), f32) * CONV_WIDTH ** -0.5
    conv_b = 0.02 * jax.random.normal(ks[6], (DEPTH, CONV_DIM), f32)
    u = jax.random.uniform(ks[7], (DEPTH, SSD_HEADS), f32)
    dt0 = jnp.exp(u * (math.log(0.1) - math.log(0.001)) + math.log(0.001))
    dt_bias = dt0 + jnp.log(-jnp.expm1(-dt0))
    a_log = jnp.log(jax.random.uniform(ks[8], (DEPTH, SSD_HEADS), f32, minval=1.0, maxval=16.0))
    d_skip = 1.0 + 0.02 * jax.random.normal(ks[9], (DEPTH, SSD_HEADS), f32)
    ssd_norm_w = 1.0 + 0.02 * jax.random.normal(ks[10], (DEPTH, SSD_W), f32)
    w_out = jax.random.normal(ks[11], (DEPTH, D_MIX, D_MODEL), f32) * D_MIX ** -0.5
    post_norm_w = 1.0 + 0.02 * jax.random.normal(ks[12], (DEPTH, D_MODEL), f32)
    return {"x": x, "pre_norm_w": pre_norm_w, "w_in": w_in, "pool_mix_w": pool_mix_w,
            "pool_scale": pool_scale, "conv_w": conv_w, "conv_b": conv_b,
            "dt_bias": dt_bias, "a_log": a_log, "d_skip": d_skip,
            "ssd_norm_w": ssd_norm_w, "w_out": w_out, "post_norm_w": post_norm_w}


def reference(x, pre_norm_w, w_in, pool_mix_w, pool_scale, conv_w, conv_b,
              dt_bias, a_log, d_skip, ssd_norm_w, w_out, post_norm_w):
    o1 = POOL_W
    o2 = o1 + POOL_W
    o3 = o2 + SSD_W
    o4 = o3 + CONV_DIM
    for layer in range(DEPTH):
        h = rms_norm(x, pre_norm_w[layer])
        proj = jnp.einsum('bld,de->ble', h, w_in[layer])
        pool_u = proj[..., :o1]
        pool_gate = proj[..., o1:o2]
        ssd_z = proj[..., o2:o3]
        ssd_xbc = proj[..., o3:o4]
        ssd_dt = proj[..., o4:]
        y_pool = pool_mixer(pool_u, pool_mix_w[layer], pool_scale[layer]) * jax.nn.silu(pool_gate)
        y_ssd = ssd_branch(ssd_z, ssd_xbc, ssd_dt, conv_w[layer], conv_b[layer],
                           dt_bias[layer], a_log[layer], d_skip[layer], ssd_norm_w[layer])
        mixed = jnp.concatenate([y_pool.astype(x.dtype), y_ssd.astype(x.dtype)], axis=-1)
        out = jnp.einsum('ble,ed->bld', mixed, w_out[layer])
        x = x + rms_norm(out, post_norm_w[layer])
    return x
```

```python
import functools

import jax
import jax.numpy as jnp
from jax import lax
from jax.experimental import pallas as pl
from jax.experimental.pallas import tpu as pltpu

F32 = jnp.float32
BF16 = jnp.bfloat16

POOL_WINDOWS = (2, 4, 8, 16)
MAX_WINDOW = max(POOL_WINDOWS)
SSD_HEAD_DIM = 64
SSD_STATE = 128
SSD_GROUPS = 4
CONV_WIDTH = 4
NORM_EPS = 1e-6

LANES = 128
SUBLANES = 8
VMEM_LIMIT = 56 * 1024 * 1024

IN_TM = 1024
IN_TN = 1024
POOL_T = 256
SSD_Q = 128
OUT_TM = 512


def _silu(v):
    return v * (0.5 * (1.0 + jnp.tanh(0.5 * v)))


def _softplus(v):
    return jnp.maximum(v, 0.0) + jnp.log1p(jnp.exp(-jnp.abs(v)))


def _in_proj_kernel(x_ref, nw_ref, w_ref, wdt_ref, o_ref, dt_ref, h_ref):
    @pl.when(pl.program_id(1) == 0)
    def _():
        x = x_ref[...]
        var = jnp.mean(x * x, axis=-1, keepdims=True)
        h = (x * lax.rsqrt(var + NORM_EPS) * nw_ref[...]).astype(BF16)
        h_ref[...] = h
        dt_ref[...] = jnp.dot(h, wdt_ref[...], preferred_element_type=F32)

    o_ref[...] = jnp.dot(h_ref[...], w_ref[...], preferred_element_type=F32).astype(o_ref.dtype)


def _in_proj(x2d, norm_w, w_main, w_dt):
    m, d = x2d.shape
    n = w_main.shape[1]
    return pl.pallas_call(
        _in_proj_kernel,
        grid=(m // IN_TM, n // IN_TN),
        in_specs=[
            pl.BlockSpec((IN_TM, d), lambda i, j: (i, 0)),
            pl.BlockSpec((1, d), lambda i, j: (0, 0)),
            pl.BlockSpec((d, IN_TN), lambda i, j: (0, j)),
            pl.BlockSpec((d, LANES), lambda i, j: (0, 0)),
        ],
        out_specs=[
            pl.BlockSpec((IN_TM, IN_TN), lambda i, j: (i, j)),
            pl.BlockSpec((IN_TM, LANES), lambda i, j: (i, 0)),
        ],
        out_shape=[
            jax.ShapeDtypeStruct((m, n), BF16),
            jax.ShapeDtypeStruct((m, LANES), F32),
        ],
        scratch_shapes=[pltpu.VMEM((IN_TM, d), BF16)],
        compiler_params=pltpu.CompilerParams(
            dimension_semantics=("arbitrary", "arbitrary"),
            vmem_limit_bytes=VMEM_LIMIT),
        name="in_proj",
    )(x2d, norm_w, w_main, w_dt)


def _pool_kernel(u_ref, gate_ref, mixw_ref, scale_ref, o_ref, tail_ref):
    t_idx = pl.program_id(1)
    tt = u_ref.shape[0]
    gw = mixw_ref.shape[1]

    @pl.when(t_idx == 0)
    def _():
        tail_ref[...] = jnp.zeros_like(tail_ref)

    row = lax.broadcasted_iota(jnp.int32, (tt, tt), 0)
    col = lax.broadcasted_iota(jnp.int32, (tt, tt), 1)
    prow = lax.broadcasted_iota(jnp.int32, (MAX_WINDOW, MAX_WINDOW), 0)
    pcol = lax.broadcasted_iota(jnp.int32, (MAX_WINDOW, MAX_WINDOW), 1)
    t_glob = t_idx * tt + lax.broadcasted_iota(jnp.int32, (tt, LANES), 0)

    for g, w in enumerate(POOL_WINDOWS):
        sl = slice(g * gw, (g + 1) * gw)
        u_g = u_ref[:, sl]
        band = jnp.where((col <= row) & (col > row - w), 1.0, 0.0).astype(BF16)
        win = jnp.dot(band, u_g, preferred_element_type=F32)
        pband = jnp.where(pcol > prow + (MAX_WINDOW - w), 1.0, 0.0).astype(BF16)
        head = jnp.dot(pband, tail_ref[:, sl], preferred_element_type=F32)
        win = jnp.concatenate([win[:MAX_WINDOW] + head, win[MAX_WINDOW:]], axis=0)
        inv = 1.0 / jnp.minimum(t_glob + 1, w).astype(F32)
        inv = jnp.concatenate([inv] * (gw // LANES), axis=1)
        pooled = (win * inv - u_g.astype(F32)).astype(BF16)
        mixed = jnp.dot(pooled, mixw_ref[g], preferred_element_type=F32)
        gate = gate_ref[:, sl].astype(F32)
        o_ref[:, sl] = (mixed * scale_ref[:, sl] * _silu(gate)).astype(o_ref.dtype)

    tail_ref[...] = u_ref[tt - MAX_WINDOW:, :]


def _pool(proj, mix_w, scale, batch, seq):
    m = proj.shape[0]
    ngroups, gw, _ = mix_w.shape
    pw = ngroups * gw
    nt = seq // POOL_T
    return pl.pallas_call(
        _pool_kernel,
        grid=(batch, nt),
        in_specs=[
            pl.BlockSpec((POOL_T, pw), lambda b, t: (b * nt + t, 0)),
            pl.BlockSpec((POOL_T, pw), lambda b, t: (b * nt + t, 1)),
            pl.BlockSpec((ngroups, gw, gw), lambda b, t: (0, 0, 0)),
            pl.BlockSpec((1, pw), lambda b, t: (0, 0)),
        ],
        out_specs=pl.BlockSpec((POOL_T, pw), lambda b, t: (b * nt + t, 0)),
        out_shape=jax.ShapeDtypeStruct((m, pw), BF16),
        scratch_shapes=[pltpu.VMEM((MAX_WINDOW, pw), BF16)],
        compiler_params=pltpu.CompilerParams(
            dimension_semantics=("arbitrary", "arbitrary"),
            vmem_limit_bytes=VMEM_LIMIT),
        name="pool",
    )(proj, proj, mix_w, scale)


def _ssd_kernel(z_ref, xbc_ref, dt_ref, cw_ref, cb_ref, dtb_ref, alog_ref, drep_ref, nw_ref,
                o_ref, xpad_ref, s_ref, y_ref):
    q = z_ref.shape[0]
    ssd_w = z_ref.shape[1]
    gn = SSD_GROUPS * SSD_STATE
    gwid = ssd_w // SSD_GROUPS
    pairs_per_group = gwid // LANES
    pad = SUBLANES

    @pl.when(pl.program_id(1) == 0)
    def _():
        xpad_ref[0:pad, :] = jnp.zeros((pad, xpad_ref.shape[1]), F32)
        s_ref[...] = jnp.zeros_like(s_ref)

    xpad_ref[pad:pad + q, :] = xbc_ref[...].astype(F32)
    acc = cb_ref[...] + cw_ref[CONV_WIDTH - 1:CONV_WIDTH, :] * xpad_ref[pad:pad + q, :]
    for k in range(CONV_WIDTH - 1):
        off = pad - (CONV_WIDTH - 1) + k
        acc = acc + cw_ref[k:k + 1, :] * xpad_ref[off:off + q, :]
    xpad_ref[0:pad, :] = xpad_ref[q:q + pad, :]
    act = _silu(acc)

    dt = _softplus(dt_ref[...] + dtb_ref[...])
    adt = dt * (-jnp.exp(alog_ref[...]))
    row = lax.broadcasted_iota(jnp.int32, (q, q), 0)
    col = lax.broadcasted_iota(jnp.int32, (q, q), 1)
    tril = row >= col
    tri = jnp.where(tril, 1.0, 0.0).astype(BF16)
    a1 = adt.astype(BF16)
    r1 = adt - a1.astype(F32)
    a2 = r1.astype(BF16)
    a3 = (r1 - a2.astype(F32)).astype(BF16)
    acs = (jnp.dot(tri, a1, preferred_element_type=F32)
           + jnp.dot(tri, a2, preferred_element_type=F32)
           + jnp.dot(tri, a3, preferred_element_type=F32))
    a_last = acs[q - 1:q, :]
    wst = dt * jnp.exp(a_last - acs)
    acs_t = acs.T
    wst_t = wst.T
    dt_t = dt.T

    lane = lax.broadcasted_iota(jnp.int32, (1, LANES), 1)
    lo = lane < SSD_HEAD_DIM

    for g in range(SSD_GROUPS):
        b_g = act[:, ssd_w + g * SSD_STATE: ssd_w + (g + 1) * SSD_STATE]
        c_g = act[:, ssd_w + gn + g * SSD_STATE: ssd_w + gn + (g + 1) * SSD_STATE]
        scores = lax.dot_general(c_g.astype(BF16), b_g.astype(BF16),
                                 (((1,), (1,)), ((), ())), preferred_element_type=F32)
        b_gt = b_g.T
        for j in range(pairs_per_group):
            pair = g * pairs_per_group + j
            xp = act[:, pair * LANES:(pair + 1) * LANES]
            sp = s_ref[pair]
            lhs_parts, rhs_parts, wl_parts, wr_parts, cds = [], [], [], [], []
            for e in range(2):
                h = 2 * pair + e
                acol = jnp.broadcast_to(acs[:, h:h + 1], (q, LANES))
                arow = jnp.broadcast_to(acs_t[h:h + 1, :], (q, q))
                decay = jnp.where(tril, jnp.exp(acol - arow), 0.0)
                m_h = scores * decay * jnp.broadcast_to(dt_t[h:h + 1, :], (q, q))
                e_h = c_g * jnp.exp(acol)
                w_h = b_gt * jnp.broadcast_to(wst_t[h:h + 1, :], (SSD_STATE, q))
                sel = lo if e == 0 else jnp.logical_not(lo)
                xm = jnp.where(sel, xp, 0.0).astype(BF16)
                sm = jnp.where(sel, sp, 0.0).astype(BF16)
                lhs_parts += [m_h.astype(BF16), e_h.astype(BF16)]
                rhs_parts += [xm, sm]
                wl_parts.append(w_h.astype(BF16))
                wr_parts.append(xm)
                cds.append(jnp.exp(acol[q - 1:q, :]))
            lhs = jnp.concatenate(lhs_parts, axis=1)
            rhs = jnp.concatenate(rhs_parts, axis=0)
            y_ref[:, pair * LANES:(pair + 1) * LANES] = jnp.dot(
                lhs, rhs, preferred_element_type=F32)
            upd = jnp.dot(jnp.concatenate(wl_parts, axis=1), jnp.concatenate(wr_parts, axis=0),
                          preferred_element_type=F32)
            s_ref[pair] = sp * jnp.where(lo, cds[0], cds[1]) + upd

    for g in range(SSD_GROUPS):
        sl = slice(g * gwid, (g + 1) * gwid)
        y = y_ref[:, sl] + drep_ref[:, sl] * act[:, sl]
        y = y * _silu(z_ref[:, sl].astype(F32))
        ms = jnp.mean(y * y, axis=-1, keepdims=True)
        o_ref[:, sl] = (y * lax.rsqrt(ms + NORM_EPS) * nw_ref[:, sl]).astype(o_ref.dtype)


def _ssd(proj, dt_raw, conv_w, conv_b, dt_bias, a_log, d_rep, norm_w, batch, seq, ssd_w):
    m = proj.shape[0]
    conv_dim = conv_w.shape[1]
    nt = seq // SSD_Q
    npairs = ssd_w // LANES
    z_blk = (proj.shape[1] - conv_dim) // ssd_w - 1
    xbc_blk = (proj.shape[1] - conv_dim) // conv_dim
    assert (z_blk + 1) * ssd_w + conv_dim == proj.shape[1]
    assert xbc_blk * conv_dim + conv_dim == proj.shape[1]
    const = lambda b, t: (0, 0)
    return pl.pallas_call(
        _ssd_kernel,
        grid=(batch, nt),
        in_specs=[
            pl.BlockSpec((SSD_Q, ssd_w), lambda b, t: (b * nt + t, z_blk)),
            pl.BlockSpec((SSD_Q, conv_dim), lambda b, t: (b * nt + t, xbc_blk)),
            pl.BlockSpec((SSD_Q, LANES), lambda b, t: (b * nt + t, 0)),
            pl.BlockSpec((CONV_WIDTH, conv_dim), const),
            pl.BlockSpec((1, conv_dim), const),
            pl.BlockSpec((1, LANES), const),
            pl.BlockSpec((1, LANES), const),
            pl.BlockSpec((1, ssd_w), const),
            pl.BlockSpec((1, ssd_w), const),
        ],
        out_specs=pl.BlockSpec((SSD_Q, ssd_w), lambda b, t: (b * nt + t, 0)),
        out_shape=jax.ShapeDtypeStruct((m, ssd_w), BF16),
        scratch_shapes=[
            pltpu.VMEM((SSD_Q + 2 * SUBLANES, conv_dim), F32),
            pltpu.VMEM((npairs, SSD_STATE, LANES), F32),
            pltpu.VMEM((SSD_Q, ssd_w), F32),
        ],
        compiler_params=pltpu.CompilerParams(
            dimension_semantics=("arbitrary", "arbitrary"),
            vmem_limit_bytes=VMEM_LIMIT),
        name="ssd",
    )(proj, proj, dt_raw, conv_w, conv_b, dt_bias, a_log, d_rep, norm_w)


def _out_proj_kernel(yp_ref, ys_ref, wp_ref, ws_ref, x_ref, nw_ref, o_ref):
    out = (jnp.dot(yp_ref[...], wp_ref[...], preferred_element_type=F32)
           + jnp.dot(ys_ref[...], ws_ref[...], preferred_element_type=F32))
    var = jnp.mean(out * out, axis=-1, keepdims=True)
    o_ref[...] = x_ref[...] + out * lax.rsqrt(var + NORM_EPS) * nw_ref[...]


def _out_proj(y_pool, y_ssd, w_pool, w_ssd, x2d, norm_w):
    m, d = x2d.shape
    kp = y_pool.shape[1]
    ks = y_ssd.shape[1]
    resident = functools.partial(pl.BlockSpec, pipeline_mode=pl.Buffered(1))
    return pl.pallas_call(
        _out_proj_kernel,
        grid=(m // OUT_TM,),
        in_specs=[
            pl.BlockSpec((OUT_TM, kp), lambda i: (i, 0)),
            pl.BlockSpec((OUT_TM, ks), lambda i: (i, 0)),
            resident((kp, d), lambda i: (0, 0)),
            resident((ks, d), lambda i: (0, 0)),
            pl.BlockSpec((OUT_TM, d), lambda i: (i, 0)),
            pl.BlockSpec((1, d), lambda i: (0, 0)),
        ],
        out_specs=pl.BlockSpec((OUT_TM, d), lambda i: (i, 0)),
        out_shape=jax.ShapeDtypeStruct((m, d), F32),
        compiler_params=pltpu.CompilerParams(
            dimension_semantics=("arbitrary",),
            vmem_limit_bytes=VMEM_LIMIT),
        name="out_proj",
    )(y_pool, y_ssd, w_pool, w_ssd, x2d, norm_w)


def _pad_lanes(v):
    return jnp.pad(v, (0, LANES - v.shape[0])).reshape(1, LANES)


def kernel(x, pre_norm_w, w_in, pool_mix_w, pool_scale, conv_w, conv_b, dt_bias, a_log, d_skip,
           ssd_norm_w, w_out, post_norm_w):
    batch, seq, d_model = x.shape
    depth = w_in.shape[0]
    pool_w = pool_scale.shape[1]
    ssd_w = ssd_norm_w.shape[1]
    heads = dt_bias.shape[1]
    n_main = w_in.shape[2] - heads
    assert ssd_w == heads * SSD_HEAD_DIM and heads <= LANES
    assert n_main == 2 * pool_w + ssd_w + conv_w.shape[2]
    assert seq % POOL_T == 0 and seq % SSD_Q == 0

    x2d = x.reshape(batch * seq, d_model)
    for layer in range(depth):
        w_main = w_in[layer, :, :n_main].astype(BF16)
        w_dt = jnp.pad(w_in[layer, :, n_main:], ((0, 0), (0, LANES - heads))).astype(BF16)
        proj, dt_raw = _in_proj(x2d, pre_norm_w[layer].reshape(1, d_model), w_main, w_dt)
        y_pool = _pool(proj, pool_mix_w[layer].astype(BF16), pool_scale[layer].reshape(1, pool_w),
                       batch, seq)
        y_ssd = _ssd(proj, dt_raw, conv_w[layer], conv_b[layer].reshape(1, -1),
                     _pad_lanes(dt_bias[layer]), _pad_lanes(a_log[layer]),
                     jnp.repeat(d_skip[layer], SSD_HEAD_DIM).reshape(1, ssd_w),
                     ssd_norm_w[layer].reshape(1, ssd_w), batch, seq, ssd_w)
        x2d = _out_proj(y_pool, y_ssd, w_out[layer, :pool_w].astype(BF16),
                        w_out[layer, pool_w:].astype(BF16), x2d,
                        post_norm_w[layer].reshape(1, d_model))
    return x2d.reshape(batch, seq, d_model)
```

```python
import functools

import jax
import jax.numpy as jnp
import numpy as np
from jax import lax
from jax.experimental import pallas as pl
from jax.experimental.pallas import tpu as pltpu

F32 = jnp.float32
BF16 = jnp.bfloat16

POOL_WINDOWS = (2, 4, 8, 16)
MAX_WINDOW = max(POOL_WINDOWS)
SSD_HEAD_DIM = 64
SSD_STATE = 128
SSD_GROUPS = 4
CONV_WIDTH = 4
NORM_EPS = 1e-6
LOG2E = 1.4426950408889634

LANES = 128
BF16_ROWS = 16
VMEM_LIMIT = 56 * 1024 * 1024

IN_TM = 1024
IN_TN = 1024
IN_CH = 256
POOL_T = 256
SSD_Q = 128
OUT_TM = 512


def _silu(v):
    hv = 0.5 * v
    return hv + hv * jnp.tanh(hv)


def _softplus(v):
    return jnp.maximum(v, 0.0) + jnp.log(1.0 + jnp.exp(-jnp.abs(v)))


def _layer_block(shape, layer, *tail):
    if tail:
        return pl.BlockSpec((None,) + shape, lambda *g: (layer,) + tuple(tail[0](*g)))
    return pl.BlockSpec((None,) + shape, lambda *g: (layer,) + (0,) * len(shape))


def _in_proj_kernel(x_ref, nw_ref, w_ref, wdt_ref, o_ref, dt_ref, h_ref):
    j = pl.program_id(1)

    @pl.when(j == 0)
    def _():
        for r in range(IN_TM // IN_CH):
            rows = slice(r * IN_CH, (r + 1) * IN_CH)
            x = x_ref[rows, :]
            var = jnp.mean(x * x, axis=-1, keepdims=True)
            h = (x * lax.rsqrt(var + NORM_EPS) * nw_ref[...]).astype(BF16)
            h_ref[rows, :] = h
            dt_ref[rows, :] = jnp.dot(h, wdt_ref[...], preferred_element_type=F32)
            o_ref[rows, :] = jnp.dot(h, w_ref[...], preferred_element_type=F32).astype(o_ref.dtype)

    @pl.when(j != 0)
    def _():
        o_ref[...] = jnp.dot(h_ref[...], w_ref[...],
                             preferred_element_type=F32).astype(o_ref.dtype)


def _in_proj(x2d, norm_w, w_in, w_dt, layer, n_main):
    m, d = x2d.shape
    return pl.pallas_call(
        _in_proj_kernel,
        grid=(m // IN_TM, n_main // IN_TN),
        in_specs=[
            pl.BlockSpec((IN_TM, d), lambda i, j: (i, 0)),
            _layer_block((1, d), layer),
            _layer_block((d, IN_TN), layer, lambda i, j: (0, j)),
            _layer_block((d, LANES), layer),
        ],
        out_specs=[
            pl.BlockSpec((IN_TM, IN_TN), lambda i, j: (i, j)),
            pl.BlockSpec((IN_TM, LANES), lambda i, j: (i, 0)),
        ],
        out_shape=[
            jax.ShapeDtypeStruct((m, n_main), BF16),
            jax.ShapeDtypeStruct((m, LANES), F32),
        ],
        scratch_shapes=[pltpu.VMEM((IN_TM, d), BF16)],
        compiler_params=pltpu.CompilerParams(
            dimension_semantics=("arbitrary", "arbitrary"),
            vmem_limit_bytes=VMEM_LIMIT),
        name="in_proj",
    )(x2d, norm_w, w_in, w_dt)


def _pool_constants(tt):
    i = np.arange(tt)[:, None]
    j = np.arange(tt)[None, :]
    pi = np.arange(MAX_WINDOW)[:, None]
    pj = np.arange(MAX_WINDOW)[None, :]
    band, pband = [], []
    for w in POOL_WINDOWS:
        inside = (j <= i) & (j > i - w)
        band.append(inside / w - (i == j))
        pband.append((pj > pi + (MAX_WINDOW - w)) / w)
    return (jnp.asarray(np.stack(band), dtype=BF16), jnp.asarray(np.stack(pband), dtype=BF16))


def _pool_kernel(u_ref, gate_ref, band_ref, pband_ref, mixw_ref, scale_ref, o_ref, tail_ref):
    t_idx = pl.program_id(1)
    tt = u_ref.shape[0]
    gw = mixw_ref.shape[1]

    @pl.when(t_idx == 0)
    def _():
        tail_ref[...] = jnp.zeros_like(tail_ref)

    def finish(pooled, rows, sl, g):
        mixed = jnp.dot(pooled.astype(BF16), mixw_ref[g], preferred_element_type=F32)
        gate = gate_ref[rows, sl].astype(F32)
        o_ref[rows, sl] = (mixed * scale_ref[:, sl] * _silu(gate)).astype(o_ref.dtype)

    for g in range(len(POOL_WINDOWS)):
        sl = slice(g * gw, (g + 1) * gw)
        pooled = jnp.dot(band_ref[g], u_ref[:, sl], preferred_element_type=F32)
        head = jnp.dot(pband_ref[g], tail_ref[:, sl], preferred_element_type=F32)
        pooled = jnp.concatenate([pooled[:MAX_WINDOW] + head, pooled[MAX_WINDOW:]], axis=0)
        finish(pooled, slice(0, tt), sl, g)

    @pl.when(t_idx == 0)
    def _():
        rows = slice(0, MAX_WINDOW)
        pi = lax.broadcasted_iota(jnp.int32, (MAX_WINDOW, MAX_WINDOW), 0)
        pj = lax.broadcasted_iota(jnp.int32, (MAX_WINDOW, MAX_WINDOW), 1)
        t_row = lax.broadcasted_iota(jnp.int32, (MAX_WINDOW, gw), 0)
        for g, w in enumerate(POOL_WINDOWS):
            sl = slice(g * gw, (g + 1) * gw)
            u0 = u_ref[rows, sl]
            ones = jnp.where((pj <= pi) & (pj > pi - w), 1.0, 0.0).astype(BF16)
            win = jnp.dot(ones, u0, preferred_element_type=F32)
            cnt = jnp.minimum(t_row + 1, w).astype(F32)
            finish(win / cnt - u0.astype(F32), rows, sl, g)

    tail_ref[...] = u_ref[tt - MAX_WINDOW:, :]


def _pool(proj, mix_w, scale, layer, batch, seq):
    m = proj.shape[0]
    _, ngroups, gw, _ = mix_w.shape
    pw = ngroups * gw
    nt = seq // POOL_T
    band, pband = _pool_constants(POOL_T)
    return pl.pallas_call(
        _pool_kernel,
        grid=(batch, nt),
        in_specs=[
            pl.BlockSpec((POOL_T, pw), lambda b, t: (b * nt + t, 0)),
            pl.BlockSpec((POOL_T, pw), lambda b, t: (b * nt + t, 1)),
            pl.BlockSpec(band.shape, lambda b, t: (0, 0, 0)),
            pl.BlockSpec(pband.shape, lambda b, t: (0, 0, 0)),
            _layer_block((ngroups, gw, gw), layer),
            _layer_block((1, pw), layer),
        ],
        out_specs=pl.BlockSpec((POOL_T, pw), lambda b, t: (b * nt + t, 0)),
        out_shape=jax.ShapeDtypeStruct((m, pw), BF16),
        scratch_shapes=[pltpu.VMEM((MAX_WINDOW, pw), BF16)],
        compiler_params=pltpu.CompilerParams(
            dimension_semantics=("arbitrary", "arbitrary"),
            vmem_limit_bytes=VMEM_LIMIT),
        name="pool",
    )(proj, proj, band, pband, mix_w, scale)


def _conv_constants(q):
    t = np.arange(q)[:, None]
    s = np.arange(q)[None, :]
    pt = np.arange(BF16_ROWS)[:, None]
    pj = np.arange(BF16_ROWS)[None, :]
    shift, pshift = [], []
    for k in range(CONV_WIDTH):
        d = CONV_WIDTH - 1 - k
        shift.append(s == t - d)
        if d:
            pshift.append((pj == BF16_ROWS - d + pt) & (pt < d))
    return (jnp.asarray(np.concatenate(shift, axis=1), dtype=BF16),
            jnp.asarray(np.concatenate(pshift, axis=1), dtype=BF16))


def _ssd_kernel(z_ref, xbc_ref, dt_ref, shift_ref, pshift_ref, cw_ref, cb_ref, dtb_ref, alog_ref,
                drep_ref, nw_ref, o_ref, tail_ref, s_ref, y_ref):
    q = z_ref.shape[0]
    ssd_w = z_ref.shape[1]
    gn = SSD_GROUPS * SSD_STATE
    gwid = ssd_w // SSD_GROUPS
    pairs_per_group = gwid // LANES

    @pl.when(pl.program_id(1) == 0)
    def _():
        tail_ref[...] = jnp.zeros_like(tail_ref)
        s_ref[...] = jnp.zeros_like(s_ref)

    def taps(frames, ntaps):
        nrow = frames.shape[0] // BF16_ROWS
        return jnp.concatenate(
            [frames[r * BF16_ROWS:(r + 1) * BF16_ROWS, :] * cw_ref[k]
             for k in range(ntaps) for r in range(nrow)], axis=0)

    xb = xbc_ref[...]
    acc = jnp.dot(shift_ref[...], taps(xb, CONV_WIDTH), preferred_element_type=F32)
    head = jnp.dot(pshift_ref[...], taps(tail_ref[...], CONV_WIDTH - 1),
                   preferred_element_type=F32)
    acc = jnp.concatenate([acc[:BF16_ROWS] + head, acc[BF16_ROWS:]], axis=0) + cb_ref[...]
    tail_ref[...] = xb[q - BF16_ROWS:, :]
    act = _silu(acc)

    dt = _softplus(dt_ref[...] + dtb_ref[...])
    adt = dt * (-jnp.exp(alog_ref[...]))
    row = lax.broadcasted_iota(jnp.int32, (q, q), 0)
    col = lax.broadcasted_iota(jnp.int32, (q, q), 1)
    tril = row >= col
    tri = jnp.where(tril, 1.0, 0.0).astype(BF16)
    a1 = adt.astype(BF16)
    r1 = adt - a1.astype(F32)
    a2 = r1.astype(BF16)
    a3 = (r1 - a2.astype(F32)).astype(BF16)
    acs2 = LOG2E * (jnp.dot(tri, a1, preferred_element_type=F32)
                    + jnp.dot(tri, a2, preferred_element_type=F32)
                    + jnp.dot(tri, a3, preferred_element_type=F32))
    bdt = acs2 - LOG2E * jnp.log(dt)
    wst = jnp.exp2(acs2[q - 1:q, :] - bdt)
    bdt_t = bdt.T
    wst_t = wst.T

    lane = lax.broadcasted_iota(jnp.int32, (q, LANES), 1)
    lo_f = lane < SSD_HEAD_DIM
    lo_b = jnp.where(lo_f, 1.0, 0.0).astype(BF16)
    hi_b = jnp.where(lo_f, 0.0, 1.0).astype(BF16)

    for g in range(SSD_GROUPS):
        b_g = act[:, ssd_w + g * SSD_STATE: ssd_w + (g + 1) * SSD_STATE]
        c_g = act[:, ssd_w + gn + g * SSD_STATE: ssd_w + gn + (g + 1) * SSD_STATE]
        scores = lax.dot_general(c_g.astype(BF16), b_g.astype(BF16),
                                 (((1,), (1,)), ((), ())), preferred_element_type=F32)
        b_gt = b_g.T
        for j in range(pairs_per_group):
            pair = g * pairs_per_group + j
            xpb = act[:, pair * LANES:(pair + 1) * LANES].astype(BF16)
            sp = s_ref[pair]
            spb = sp.astype(BF16)
            x_lo, x_hi = xpb * lo_b, xpb * hi_b
            s_lo, s_hi = spb * lo_b, spb * hi_b
            lhs_parts, wl_parts, cds = [], [], []
            for e in range(2):
                h = 2 * pair + e
                acol = jnp.broadcast_to(acs2[:, h:h + 1], (q, LANES))
                decay = jnp.where(tril, jnp.exp2(acol - bdt_t[h:h + 1, :]), 0.0)
                lhs_parts += [(scores * decay).astype(BF16),
                              (c_g * jnp.exp2(acol)).astype(BF16)]
                wl_parts.append((b_gt * wst_t[h:h + 1, :]).astype(BF16))
                cds.append(jnp.exp2(acol[q - 1:q, :]))
            lhs = jnp.concatenate(lhs_parts, axis=1)
            rhs = jnp.concatenate([x_lo, s_lo, x_hi, s_hi], axis=0)
            y_ref[:, pair * LANES:(pair + 1) * LANES] = jnp.dot(
                lhs, rhs, preferred_element_type=F32)
            upd = jnp.dot(jnp.concatenate(wl_parts, axis=1),
                          jnp.concatenate([x_lo, x_hi], axis=0),
                          preferred_element_type=F32)
            s_ref[pair] = sp * jnp.where(lo_f[:1], cds[0], cds[1]) + upd

    for g in range(SSD_GROUPS):
        sl = slice(g * gwid, (g + 1) * gwid)
        y = y_ref[:, sl] + drep_ref[:, sl] * act[:, sl]
        y = y * _silu(z_ref[:, sl].astype(F32))
        ms = jnp.mean(y * y, axis=-1, keepdims=True)
        o_ref[:, sl] = (y * lax.rsqrt(ms + NORM_EPS) * nw_ref[:, sl]).astype(o_ref.dtype)


def _ssd(proj, dt_raw, conv_w16, conv_b, dt_bias, a_log, d_rep, norm_w, layer, batch, seq, ssd_w):
    m = proj.shape[0]
    conv_dim = conv_b.shape[-1]
    nt = seq // SSD_Q
    npairs = ssd_w // LANES
    z_blk = (proj.shape[1] - conv_dim) // ssd_w - 1
    xbc_blk = (proj.shape[1] - conv_dim) // conv_dim
    assert (z_blk + 1) * ssd_w + conv_dim == proj.shape[1]
    assert xbc_blk * conv_dim + conv_dim == proj.shape[1]
    shift, pshift = _conv_constants(SSD_Q)
    const = lambda b, t: (0, 0)
    return pl.pallas_call(
        _ssd_kernel,
        grid=(batch, nt),
        in_specs=[
            pl.BlockSpec((SSD_Q, ssd_w), lambda b, t: (b * nt + t, z_blk)),
            pl.BlockSpec((SSD_Q, conv_dim), lambda b, t: (b * nt + t, xbc_blk)),
            pl.BlockSpec((SSD_Q, LANES), lambda b, t: (b * nt + t, 0)),
            pl.BlockSpec(shift.shape, const),
            pl.BlockSpec(pshift.shape, const),
            _layer_block((CONV_WIDTH, BF16_ROWS, conv_dim), layer),
            _layer_block((1, conv_dim), layer),
            _layer_block((1, LANES), layer),
            _layer_block((1, LANES), layer),
            _layer_block((1, ssd_w), layer),
            _layer_block((1, ssd_w), layer),
        ],
        out_specs=pl.BlockSpec((SSD_Q, ssd_w), lambda b, t: (b * nt + t, 0)),
        out_shape=jax.ShapeDtypeStruct((m, ssd_w), BF16),
        scratch_shapes=[
            pltpu.VMEM((BF16_ROWS, conv_dim), BF16),
            pltpu.VMEM((npairs, SSD_STATE, LANES), F32),
            pltpu.VMEM((SSD_Q, ssd_w), F32),
        ],
        compiler_params=pltpu.CompilerParams(
            dimension_semantics=("arbitrary", "arbitrary"),
            vmem_limit_bytes=VMEM_LIMIT),
        name="ssd",
    )(proj, proj, dt_raw, shift, pshift, conv_w16, conv_b, dt_bias, a_log, d_rep, norm_w)


def _out_proj_kernel(yp_ref, ys_ref, wp_ref, ws_ref, x_ref, nw_ref, o_ref):
    out = (jnp.dot(yp_ref[...], wp_ref[...], preferred_element_type=F32)
           + jnp.dot(ys_ref[...], ws_ref[...], preferred_element_type=F32))
    var = jnp.mean(out * out, axis=-1, keepdims=True)
    o_ref[...] = x_ref[...] + out * lax.rsqrt(var + NORM_EPS) * nw_ref[...]


def _out_proj(y_pool, y_ssd, w_out, x2d, norm_w, layer):
    m, d = x2d.shape
    kp = y_pool.shape[1]
    ks = y_ssd.shape[1]
    assert kp == ks and w_out.shape[1] == kp + ks
    resident = dict(pipeline_mode=pl.Buffered(1))
    return pl.pallas_call(
        _out_proj_kernel,
        grid=(m // OUT_TM,),
        in_specs=[
            pl.BlockSpec((OUT_TM, kp), lambda i: (i, 0)),
            pl.BlockSpec((OUT_TM, ks), lambda i: (i, 0)),
            pl.BlockSpec((None, kp, d), lambda i: (layer, 0, 0), **resident),
            pl.BlockSpec((None, ks, d), lambda i: (layer, 1, 0), **resident),
            pl.BlockSpec((OUT_TM, d), lambda i: (i, 0)),
            _layer_block((1, d), layer),
        ],
        out_specs=pl.BlockSpec((OUT_TM, d), lambda i: (i, 0)),
        out_shape=jax.ShapeDtypeStruct((m, d), F32),
        compiler_params=pltpu.CompilerParams(
            dimension_semantics=("arbitrary",),
            vmem_limit_bytes=VMEM_LIMIT),
        name="out_proj",
    )(y_pool, y_ssd, w_out, w_out, x2d, norm_w)


def kernel(x, pre_norm_w, w_in, pool_mix_w, pool_scale, conv_w, conv_b, dt_bias, a_log, d_skip,
           ssd_norm_w, w_out, post_norm_w):
    batch, seq, d_model = x.shape
    depth = w_in.shape[0]
    pool_w = pool_scale.shape[1]
    ssd_w = ssd_norm_w.shape[1]
    heads = dt_bias.shape[1]
    conv_dim = conv_w.shape[2]
    n_main = w_in.shape[2] - heads
    assert ssd_w == heads * SSD_HEAD_DIM and heads <= LANES
    assert n_main == 2 * pool_w + ssd_w + conv_dim and n_main % IN_TN == 0
    assert seq % POOL_T == 0 and seq % SSD_Q == 0

    head_pad = ((0, 0), (0, LANES - heads))
    w_in_b = w_in.astype(BF16)
    w_dt_b = jnp.pad(w_in[:, :, n_main:], ((0, 0),) + head_pad).astype(BF16)
    w_out_b = w_out.astype(BF16)
    mix_b = pool_mix_w.astype(BF16)
    conv_w16 = jnp.broadcast_to(conv_w.astype(BF16)[:, :, None, :],
                                (depth, CONV_WIDTH, BF16_ROWS, conv_dim))
    rows = lambda p: p.reshape(depth, 1, -1)
    pre_w, post_w = rows(pre_norm_w), rows(post_norm_w)
    scale, conv_b3, ssd_nw = rows(pool_scale), rows(conv_b), rows(ssd_norm_w)
    dt_bias3, a_log3 = rows(jnp.pad(dt_bias, head_pad)), rows(jnp.pad(a_log, head_pad))
    d_rep = rows(jnp.repeat(d_skip, SSD_HEAD_DIM, axis=1))

    x2d = x.reshape(batch * seq, d_model)
    for layer in range(depth):
        proj, dt_raw = _in_proj(x2d, pre_w, w_in_b, w_dt_b, layer, n_main)
        y_pool = _pool(proj, mix_b, scale, layer, batch, seq)
        y_ssd = _ssd(proj, dt_raw, conv_w16, conv_b3, dt_bias3, a_log3, d_rep, ssd_nw,
                     layer, batch, seq, ssd_w)
        x2d = _out_proj(y_pool, y_ssd, w_out_b, x2d, post_w, layer)
    return x2d.reshape(batch, seq, d_model)
```

```python
import functools

import jax
import jax.numpy as jnp
import numpy as np
from jax import lax
from jax.experimental import pallas as pl
from jax.experimental.pallas import tpu as pltpu

F32 = jnp.float32
BF16 = jnp.bfloat16

POOL_WINDOWS = (2, 4, 8, 16)
MAX_WINDOW = max(POOL_WINDOWS)
SSD_HEAD_DIM = 64
SSD_STATE = 128
SSD_GROUPS = 4
CONV_WIDTH = 4
NORM_EPS = 1e-6
LOG2E = 1.4426950408889634

LANES = 128
BF16_ROWS = 16
VMEM_LIMIT = 56 * 1024 * 1024

IN_TM = 1024
IN_TN = 1024
IN_CH = 256
MIX_TM = 256
SSD_Q = 128

NT_DIMS = (((1,), (1,)), ((), ()))


def _silu(v):
    hv = 0.5 * v
    return hv + hv * jnp.tanh(hv)


def _softplus(v):
    return jnp.maximum(v, 0.0) + jnp.log(1.0 + jnp.exp(-jnp.abs(v)))


def _layer_block(shape, layer, *tail):
    if tail:
        return pl.BlockSpec((None,) + shape, lambda *g: (layer,) + tuple(tail[0](*g)))
    return pl.BlockSpec((None,) + shape, lambda *g: (layer,) + (0,) * len(shape))


def _const_block(arr):
    return pl.BlockSpec(arr.shape, lambda *g: (0,) * arr.ndim)


def _in_proj_kernel(x_ref, nw_ref, w_ref, wdt_ref, o_ref, dt_ref, h_ref):
    j = pl.program_id(1)

    @pl.when(j == 0)
    def _():
        for r in range(IN_TM // IN_CH):
            rows = slice(r * IN_CH, (r + 1) * IN_CH)
            x = x_ref[rows, :]
            var = jnp.mean(x * x, axis=-1, keepdims=True)
            h = (x * lax.rsqrt(var + NORM_EPS) * nw_ref[...]).astype(BF16)
            h_ref[rows, :] = h
            dt_ref[rows, :] = lax.dot_general(h, wdt_ref[...], NT_DIMS, preferred_element_type=F32)
            o_ref[rows, :] = lax.dot_general(h, w_ref[...], NT_DIMS,
                                             preferred_element_type=F32).astype(o_ref.dtype)

    @pl.when(j != 0)
    def _():
        o_ref[...] = lax.dot_general(h_ref[...], w_ref[...], NT_DIMS,
                                     preferred_element_type=F32).astype(o_ref.dtype)


def _in_proj(x2d, norm_w, w_in_t, w_dt_t, layer, n_main):
    m, d = x2d.shape
    return pl.pallas_call(
        _in_proj_kernel,
        grid=(m // IN_TM, n_main // IN_TN),
        in_specs=[
            pl.BlockSpec((IN_TM, d), lambda i, j: (i, 0)),
            _layer_block((1, d), layer),
            _layer_block((IN_TN, d), layer, lambda i, j: (j, 0)),
            _layer_block((LANES, d), layer),
        ],
        out_specs=[
            pl.BlockSpec((IN_TM, IN_TN), lambda i, j: (i, j)),
            pl.BlockSpec((IN_TM, LANES), lambda i, j: (i, 0)),
        ],
        out_shape=[
            jax.ShapeDtypeStruct((m, n_main), BF16),
            jax.ShapeDtypeStruct((m, LANES), F32),
        ],
        scratch_shapes=[pltpu.VMEM((IN_TM, d), BF16)],
        compiler_params=pltpu.CompilerParams(
            dimension_semantics=("arbitrary", "arbitrary"),
            vmem_limit_bytes=VMEM_LIMIT),
        name="in_proj",
    )(x2d, norm_w, w_in_t, w_dt_t)


def _pool_constants(tt, gw):
    i = np.arange(tt)[:, None]
    j = np.arange(tt)[None, :]
    pi = np.arange(MAX_WINDOW)[:, None]
    pj = np.arange(MAX_WINDOW)[None, :]
    band, pband, fix = [], [], []
    for w in POOL_WINDOWS:
        inside = (j <= i) & (j > i - w)
        band.append(inside / w - (i == j))
        pband.append((pj > pi + (MAX_WINDOW - w)) / w)
        fix.append(np.repeat(1.0 / np.minimum(pi + 1, w) - 1.0 / w, gw, axis=1))
    return (jnp.asarray(np.stack(band), dtype=BF16), jnp.asarray(np.stack(pband), dtype=BF16),
            jnp.asarray(pj <= pi, dtype=BF16), jnp.asarray(np.concatenate(fix, axis=1), dtype=F32))


def _conv_constants(q):
    t = np.arange(q)[:, None]
    s = np.arange(q)[None, :]
    pt = np.arange(BF16_ROWS)[:, None]
    pj = np.arange(BF16_ROWS)[None, :]
    shift, pshift = [], []
    for k in range(CONV_WIDTH):
        d = CONV_WIDTH - 1 - k
        shift.append(s == t - d)
        if d:
            pshift.append((pj == BF16_ROWS - d + pt) & (pt < d))
    return (jnp.asarray(np.concatenate(shift, axis=1), dtype=BF16),
            jnp.asarray(np.concatenate(pshift, axis=1), dtype=BF16))


def _pool_part(u_ref, gate_ref, band_ref, pband_ref, tril_ref, fix_ref, mixw_ref, scale_ref,
               ptail_ref, yp_ref, first):
    gw = mixw_ref.shape[1]
    head_rows = slice(0, MAX_WINDOW)
    for g in range(len(POOL_WINDOWS)):
        sl = slice(g * gw, (g + 1) * gw)
        pooled = jnp.dot(band_ref[g], u_ref[:, sl], preferred_element_type=F32)
        head = jnp.dot(pband_ref[g], ptail_ref[:, sl], preferred_element_type=F32)
        start = jnp.dot(tril_ref[...], u_ref[head_rows, sl], preferred_element_type=F32)
        head = head + start * (first * fix_ref[:, sl])
        pooled = jnp.concatenate([pooled[:MAX_WINDOW] + head, pooled[MAX_WINDOW:]], axis=0)
        mixed = jnp.dot(pooled.astype(BF16), mixw_ref[g], preferred_element_type=F32)
        gate = gate_ref[:, sl].astype(F32)
        yp_ref[:, sl] = (mixed * scale_ref[:, sl] * _silu(gate)).astype(yp_ref.dtype)


def _ssd_chunk(c, z_ref, xbc_ref, dt_ref, shift_ref, pshift_ref, cw_ref, cb_ref, dtb_ref, alog_ref,
               drep_ref, nw_ref, ctail_ref, s_ref, y_ref, ys_ref, side_work):
    q = SSD_Q
    ssd_w = z_ref.shape[1]
    gn = SSD_GROUPS * SSD_STATE
    gwid = ssd_w // SSD_GROUPS
    pairs_per_group = gwid // LANES
    rows = slice(c * q, (c + 1) * q)

    def taps(frames, ntaps):
        nrow = frames.shape[0] // BF16_ROWS
        return jnp.concatenate(
            [frames[r * BF16_ROWS:(r + 1) * BF16_ROWS, :] * cw_ref[k]
             for k in range(ntaps) for r in range(nrow)], axis=0)

    before = ctail_ref[...] if c == 0 else xbc_ref[c * q - BF16_ROWS:c * q, :]
    acc = jnp.dot(shift_ref[...], taps(xbc_ref[rows, :], CONV_WIDTH), preferred_element_type=F32)
    head = jnp.dot(pshift_ref[...], taps(before, CONV_WIDTH - 1), preferred_element_type=F32)
    acc = jnp.concatenate([acc[:BF16_ROWS] + head, acc[BF16_ROWS:]], axis=0) + cb_ref[...]
    act = _silu(acc)

    dt = _softplus(dt_ref[rows, :] + dtb_ref[...])
    adt = dt * (-jnp.exp(alog_ref[...]))
    tril = (lax.broadcasted_iota(jnp.int32, (q, q), 0)
            >= lax.broadcasted_iota(jnp.int32, (q, q), 1))
    tri = jnp.where(tril, 1.0, 0.0).astype(BF16)
    a1 = adt.astype(BF16)
    r1 = adt - a1.astype(F32)
    a2 = r1.astype(BF16)
    a3 = (r1 - a2.astype(F32)).astype(BF16)
    acs2 = LOG2E * (jnp.dot(tri, a1, preferred_element_type=F32)
                    + jnp.dot(tri, a2, preferred_element_type=F32)
                    + jnp.dot(tri, a3, preferred_element_type=F32))
    bdt = acs2 - LOG2E * jnp.log(dt)
    wst = jnp.exp2(acs2[q - 1:q, :] - bdt)
    bdt_t = bdt.T
    wst_t = wst.T

    lo_f = lax.broadcasted_iota(jnp.int32, (q, LANES), 1) < SSD_HEAD_DIM
    lo_b = jnp.where(lo_f, 1.0, 0.0).astype(BF16)
    hi_b = jnp.where(lo_f, 0.0, 1.0).astype(BF16)

    for g in range(SSD_GROUPS):
        side_work()
        b_g = act[:, ssd_w + g * SSD_STATE: ssd_w + (g + 1) * SSD_STATE]
        c_g = act[:, ssd_w + gn + g * SSD_STATE: ssd_w + gn + (g + 1) * SSD_STATE]
        scores = lax.dot_general(c_g.astype(BF16), b_g.astype(BF16), NT_DIMS,
                                 preferred_element_type=F32)
        b_gt = b_g.T
        for j in range(pairs_per_group):
            pair = g * pairs_per_group + j
            xpb = act[:, pair * LANES:(pair + 1) * LANES].astype(BF16)
            sp = s_ref[pair]
            spb = sp.astype(BF16)
            x_lo, x_hi = xpb * lo_b, xpb * hi_b
            s_lo, s_hi = spb * lo_b, spb * hi_b
            lhs_parts, wl_parts, cds = [], [], []
            for e in range(2):
                h = 2 * pair + e
                acol = jnp.broadcast_to(acs2[:, h:h + 1], (q, LANES))
                decay = jnp.where(tril, jnp.exp2(acol - bdt_t[h:h + 1, :]), 0.0)
                lhs_parts += [(scores * decay).astype(BF16),
                              (c_g * jnp.exp2(acol)).astype(BF16)]
                wl_parts.append((b_gt * wst_t[h:h + 1, :]).astype(BF16))
                cds.append(jnp.exp2(acol[q - 1:q, :]))
            lhs = jnp.concatenate(lhs_parts, axis=1)
            rhs = jnp.concatenate([x_lo, s_lo, x_hi, s_hi], axis=0)
            y_ref[c, :, pair * LANES:(pair + 1) * LANES] = jnp.dot(
                lhs, rhs, preferred_element_type=F32)
            upd = jnp.dot(jnp.concatenate(wl_parts, axis=1),
                          jnp.concatenate([x_lo, x_hi], axis=0),
                          preferred_element_type=F32)
            s_ref[pair] = sp * jnp.where(lo_f[:1], cds[0], cds[1]) + upd

    for g in range(SSD_GROUPS):
        sl = slice(g * gwid, (g + 1) * gwid)
        y = y_ref[c, :, sl] + drep_ref[:, sl] * act[:, sl]
        y = y * _silu(z_ref[rows, sl].astype(F32))
        ms = jnp.mean(y * y, axis=-1, keepdims=True)
        ys_ref[rows, sl] = (y * lax.rsqrt(ms + NORM_EPS) * nw_ref[:, sl]).astype(ys_ref.dtype)


def _mixer_out_kernel(u_ref, gate_ref, z_ref, xbc_ref, dt_ref, x_ref,
                      band_ref, pband_ref, tril_ref, fix_ref, shift_ref, pshift_ref,
                      mixw_ref, scale_ref, cw_ref, cb_ref, dtb_ref, alog_ref, drep_ref, snw_ref,
                      wp_ref, ws_ref, pnw_ref,
                      o_ref,
                      ptail_ref, ctail_ref, s_ref, y_ref, yp_ref, ys_ref, ypc_ref, ysc_ref, out_ref,
                      *, tiles_per_seq):
    tt = u_ref.shape[0]
    i = pl.program_id(0)
    seq_start = (i % tiles_per_seq) == 0

    @pl.when(i == 0)
    def _():
        yp_ref[...] = jnp.zeros_like(yp_ref)
        ys_ref[...] = jnp.zeros_like(ys_ref)

    @pl.when(seq_start)
    def _():
        ptail_ref[...] = jnp.zeros_like(ptail_ref)
        ctail_ref[...] = jnp.zeros_like(ctail_ref)
        s_ref[...] = jnp.zeros_like(s_ref)

    ypc_ref[...] = yp_ref[...]
    ysc_ref[...] = ys_ref[...]

    nchunks = tt // SSD_Q
    n_slabs = nchunks * SSD_GROUPS
    slab = o_ref.shape[1] // n_slabs
    assert slab % (2 * LANES) == 0 and slab * n_slabs == o_ref.shape[1]
    emitted = []

    def out_proj_slab():
        k = len(emitted)
        emitted.append(k)
        cols = slice(k * slab, (k + 1) * slab)
        out_ref[:, cols] = (jnp.dot(ypc_ref[...], wp_ref[:, cols], preferred_element_type=F32)
                            + jnp.dot(ysc_ref[...], ws_ref[:, cols], preferred_element_type=F32))

    first = jnp.where(seq_start, 1.0, 0.0).astype(F32)
    _pool_part(u_ref, gate_ref, band_ref, pband_ref, tril_ref, fix_ref, mixw_ref, scale_ref,
               ptail_ref, yp_ref, first)
    for c in range(nchunks):
        _ssd_chunk(c, z_ref, xbc_ref, dt_ref, shift_ref, pshift_ref, cw_ref, cb_ref, dtb_ref,
                   alog_ref, drep_ref, snw_ref, ctail_ref, s_ref, y_ref, ys_ref, out_proj_slab)
    assert len(emitted) == n_slabs
    ptail_ref[...] = u_ref[tt - MAX_WINDOW:, :]
    ctail_ref[...] = xbc_ref[tt - BF16_ROWS:, :]

    out = out_ref[...]
    var = jnp.mean(out * out, axis=-1, keepdims=True)
    o_ref[...] = x_ref[...] + out * lax.rsqrt(var + NORM_EPS) * pnw_ref[...]


def _mixer_out(proj, dt_raw, x2d, mix_w, scale, conv_w16, conv_b, dt_bias, a_log, d_rep, ssd_nw,
               w_out, post_w, layer, batch, seq):
    m, d = x2d.shape
    _, ngroups, gw, _ = mix_w.shape
    pw = ngroups * gw
    ssd_w = ssd_nw.shape[-1]
    conv_dim = conv_b.shape[-1]
    assert pw == ssd_w and proj.shape[1] == 2 * pw + ssd_w + conv_dim and conv_dim * 2 == 3 * pw
    assert w_out.shape[1] == pw + ssd_w and MIX_TM % SSD_Q == 0
    nt = seq // MIX_TM
    ntiles = batch * nt
    npairs = ssd_w // LANES
    band, pband, tril, fix = _pool_constants(MIX_TM, gw)
    shift, pshift = _conv_constants(SSD_Q)
    row_blk = lambda col: (lambda i: (jnp.minimum(i, ntiles - 1), col))
    prev_blk = lambda i: (jnp.maximum(i - 1, 0), 0)
    resident = dict(pipeline_mode=pl.Buffered(1))
    return pl.pallas_call(
        functools.partial(_mixer_out_kernel, tiles_per_seq=nt),
        grid=(ntiles + 1,),
        in_specs=[
            pl.BlockSpec((MIX_TM, pw), row_blk(0)),
            pl.BlockSpec((MIX_TM, pw), row_blk(1)),
            pl.BlockSpec((MIX_TM, ssd_w), row_blk(2)),
            pl.BlockSpec((MIX_TM, conv_dim), row_blk(2)),
            pl.BlockSpec((MIX_TM, LANES), row_blk(0)),
            pl.BlockSpec((MIX_TM, d), prev_blk),
            _const_block(band), _const_block(pband), _const_block(tril), _const_block(fix),
            _const_block(shift), _const_block(pshift),
            _layer_block((ngroups, gw, gw), layer),
            _layer_block((1, pw), layer),
            _layer_block((CONV_WIDTH, BF16_ROWS, conv_dim), layer),
            _layer_block((1, conv_dim), layer),
            _layer_block((1, LANES), layer),
            _layer_block((1, LANES), layer),
            _layer_block((1, ssd_w), layer),
            _layer_block((1, ssd_w), layer),
            pl.BlockSpec((None, pw, d), lambda i: (layer, 0, 0), **resident),
            pl.BlockSpec((None, ssd_w, d), lambda i: (layer, 1, 0), **resident),
            _layer_block((1, d), layer),
        ],
        out_specs=pl.BlockSpec((MIX_TM, d), prev_blk),
        out_shape=jax.ShapeDtypeStruct((m, d), F32),
        scratch_shapes=[
            pltpu.VMEM((MAX_WINDOW, pw), BF16),
            pltpu.VMEM((BF16_ROWS, conv_dim), BF16),
            pltpu.VMEM((npairs, SSD_STATE, LANES), F32),
            pltpu.VMEM((MIX_TM // SSD_Q, SSD_Q, ssd_w), F32),
            pltpu.VMEM((MIX_TM, pw), BF16),
            pltpu.VMEM((MIX_TM, ssd_w), BF16),
            pltpu.VMEM((MIX_TM, pw), BF16),
            pltpu.VMEM((MIX_TM, ssd_w), BF16),
            pltpu.VMEM((MIX_TM, d), F32),
        ],
        compiler_params=pltpu.CompilerParams(
            dimension_semantics=("arbitrary",),
            vmem_limit_bytes=VMEM_LIMIT),
        name="mixer_out",
    )(proj, proj, proj, proj, dt_raw, x2d, band, pband, tril, fix, shift, pshift,
      mix_w, scale, conv_w16, conv_b, dt_bias, a_log, d_rep, ssd_nw, w_out, w_out, post_w)


def kernel(x, pre_norm_w, w_in, pool_mix_w, pool_scale, conv_w, conv_b, dt_bias, a_log, d_skip,
           ssd_norm_w, w_out, post_norm_w):
    batch, seq, d_model = x.shape
    depth = w_in.shape[0]
    pool_w = pool_scale.shape[1]
    ssd_w = ssd_norm_w.shape[1]
    heads = dt_bias.shape[1]
    conv_dim = conv_w.shape[2]
    n_main = w_in.shape[2] - heads
    assert ssd_w == heads * SSD_HEAD_DIM and heads <= LANES
    assert n_main == 2 * pool_w + ssd_w + conv_dim and n_main % IN_TN == 0
    assert seq % MIX_TM == 0 and (batch * seq) % IN_TM == 0

    head_pad = ((0, 0), (0, LANES - heads))
    w_in_t = jnp.swapaxes(w_in, 1, 2)
    w_main_t = w_in_t.astype(BF16)
    w_dt_t = jnp.pad(w_in_t[:, n_main:, :], ((0, 0), (0, LANES - heads), (0, 0))).astype(BF16)
    w_out_b = w_out.astype(BF16)
    mix_b = pool_mix_w.astype(BF16)
    conv_w16 = jnp.broadcast_to(conv_w.astype(BF16)[:, :, None, :],
                                (depth, CONV_WIDTH, BF16_ROWS, conv_dim))
    rows = lambda p: p.reshape(depth, 1, -1)
    pre_w, post_w = rows(pre_norm_w), rows(post_norm_w)
    scale, conv_b3, ssd_nw = rows(pool_scale), rows(conv_b), rows(ssd_norm_w)
    dt_bias3, a_log3 = rows(jnp.pad(dt_bias, head_pad)), rows(jnp.pad(a_log, head_pad))
    d_rep = rows(jnp.repeat(d_skip, SSD_HEAD_DIM, axis=1))

    x2d = x.reshape(batch * seq, d_model)
    for layer in range(depth):
        proj, dt_raw = _in_proj(x2d, pre_w, w_main_t, w_dt_t, layer, n_main)
        x2d = _mixer_out(proj, dt_raw, x2d, mix_b, scale, conv_w16, conv_b3, dt_bias3, a_log3,
                         d_rep, ssd_nw, w_out_b, post_w, layer, batch, seq)
    return x2d.reshape(batch, seq, d_model)
```

```python
import functools

import jax
import jax.numpy as jnp
import numpy as np
from jax import lax
from jax.experimental import pallas as pl
from jax.experimental.pallas import tpu as pltpu

F32 = jnp.float32
BF16 = jnp.bfloat16

POOL_WINDOWS = (2, 4, 8, 16)
MAX_WINDOW = max(POOL_WINDOWS)
SSD_HEAD_DIM = 64
SSD_STATE = 128
SSD_GROUPS = 4
CONV_WIDTH = 4
NORM_EPS = 1e-6
LOG2E = 1.4426950408889634

LANES = 128
BF16_ROWS = 16
VMEM_LIMIT = 56 * 1024 * 1024

IN_TM = 1024
IN_TN = 1024
IN_CH = 256
MIX_TM = 256
SSD_Q = 128

SLAB_SLOTS = (("ssd", 0, 0), ("ssd", 0, 1), ("ssd", 0, 2), ("ssd", 0, 3),
              ("ssd", 1, 0), ("ssd", 1, 1), ("ssd", 1, 2), ("pool", 0))

NT_DIMS = (((1,), (1,)), ((), ()))


def _silu(v):
    hv = 0.5 * v
    return hv + hv * jnp.tanh(hv)


def _softplus(v):
    return jnp.maximum(v, 0.0) + jnp.log(1.0 + jnp.exp(-jnp.abs(v)))


def _layer_block(shape, layer, *tail):
    if tail:
        return pl.BlockSpec((None,) + shape, lambda *g: (layer,) + tuple(tail[0](*g)))
    return pl.BlockSpec((None,) + shape, lambda *g: (layer,) + (0,) * len(shape))


def _const_block(arr):
    return pl.BlockSpec(arr.shape, lambda *g: (0,) * arr.ndim)


def _in_proj_kernel(x_ref, nw_ref, w_ref, wdt_ref, o_ref, dt_ref, h_ref):
    j = pl.program_id(1)

    @pl.when(j == 0)
    def _():
        for r in range(IN_TM // IN_CH):
            rows = slice(r * IN_CH, (r + 1) * IN_CH)
            x = x_ref[rows, :]
            var = jnp.mean(x * x, axis=-1, keepdims=True)
            h = (x * lax.rsqrt(var + NORM_EPS) * nw_ref[...]).astype(BF16)
            h_ref[rows, :] = h
            dt_ref[rows, :] = lax.dot_general(h, wdt_ref[...], NT_DIMS, preferred_element_type=F32)
            o_ref[rows, :] = lax.dot_general(h, w_ref[...], NT_DIMS,
                                             preferred_element_type=F32).astype(o_ref.dtype)

    @pl.when(j != 0)
    def _():
        o_ref[...] = lax.dot_general(h_ref[...], w_ref[...], NT_DIMS,
                                     preferred_element_type=F32).astype(o_ref.dtype)


def _in_proj(x2d, norm_w, w_in_t, w_dt_t, layer, n_main):
    m, d = x2d.shape
    return pl.pallas_call(
        _in_proj_kernel,
        grid=(m // IN_TM, n_main // IN_TN),
        in_specs=[
            pl.BlockSpec((IN_TM, d), lambda i, j: (i, 0)),
            _layer_block((1, d), layer),
            _layer_block((IN_TN, d), layer, lambda i, j: (j, 0)),
            _layer_block((LANES, d), layer),
        ],
        out_specs=[
            pl.BlockSpec((IN_TM, IN_TN), lambda i, j: (i, j)),
            pl.BlockSpec((IN_TM, LANES), lambda i, j: (i, 0)),
        ],
        out_shape=[
            jax.ShapeDtypeStruct((m, n_main), BF16),
            jax.ShapeDtypeStruct((m, LANES), F32),
        ],
        scratch_shapes=[pltpu.VMEM((IN_TM, d), BF16)],
        compiler_params=pltpu.CompilerParams(
            dimension_semantics=("arbitrary", "arbitrary"),
            vmem_limit_bytes=VMEM_LIMIT),
        name="in_proj",
    )(x2d, norm_w, w_in_t, w_dt_t)


def _fold_kernel(wu_ref, mix_ref, scale_ref, w_any_ref, o_ref):
    del w_any_ref
    wm = (mix_ref[...] * scale_ref[...]).astype(BF16)
    o_ref[...] = lax.dot_general(wm, wu_ref[...].astype(BF16), (((0,), (0,)), ((), ())),
                                 preferred_element_type=F32).astype(o_ref.dtype)


def _fold_pool_map(w_in_t, w_main_t, mix_w, scale):
    depth, _, d = w_in_t.shape
    _, ngroups, gw, _ = mix_w.shape
    rows_blk = pl.BlockSpec((None, gw, d), lambda l, g: (l, g, 0))
    return pl.pallas_call(
        _fold_kernel,
        grid=(depth, ngroups),
        in_specs=[
            rows_blk,
            pl.BlockSpec((None, None, gw, gw), lambda l, g: (l, g, 0, 0)),
            pl.BlockSpec((None, 1, gw), lambda l, g: (l, 0, g)),
            pl.BlockSpec(memory_space=pl.ANY),
        ],
        out_specs=rows_blk,
        out_shape=jax.ShapeDtypeStruct(w_main_t.shape, w_main_t.dtype),
        input_output_aliases={3: 0},
        compiler_params=pltpu.CompilerParams(
            dimension_semantics=("arbitrary", "arbitrary"),
            vmem_limit_bytes=VMEM_LIMIT),
        name="fold_pool_map",
    )(w_in_t, mix_w, scale, w_main_t)


def _pool_constants(tt, gw):
    i = np.arange(tt)[:, None]
    j = np.arange(tt)[None, :]
    pi = np.arange(MAX_WINDOW)[:, None]
    pj = np.arange(MAX_WINDOW)[None, :]
    band, pband, fix = [], [], []
    for w in POOL_WINDOWS:
        inside = (j <= i) & (j > i - w)
        band.append(inside / w - (i == j))
        pband.append((pj > pi + (MAX_WINDOW - w)) / w)
        fix.append(np.repeat(1.0 / np.minimum(pi + 1, w) - 1.0 / w, gw, axis=1))
    return (jnp.asarray(np.stack(band), dtype=BF16), jnp.asarray(np.stack(pband), dtype=BF16),
            jnp.asarray(pj <= pi, dtype=BF16), jnp.asarray(np.concatenate(fix, axis=1), dtype=F32))


def _conv_constants(q):
    t = np.arange(q)[:, None]
    s = np.arange(q)[None, :]
    pt = np.arange(BF16_ROWS)[:, None]
    pj = np.arange(BF16_ROWS)[None, :]
    shift, pshift = [], []
    for k in range(CONV_WIDTH):
        d = CONV_WIDTH - 1 - k
        shift.append(s == t - d)
        if d:
            pshift.append((pj == BF16_ROWS - d + pt) & (pt < d))
    return (jnp.asarray(np.concatenate(shift, axis=1), dtype=BF16),
            jnp.asarray(np.concatenate(pshift, axis=1), dtype=BF16))


def _pool_part(u_ref, gate_ref, band_ref, pband_ref, tril_ref, fix_ref, ptail_ref, yp_ref, first,
               side_work):
    gw = u_ref.shape[1] // len(POOL_WINDOWS)
    head_rows = slice(0, MAX_WINDOW)
    for g in range(len(POOL_WINDOWS)):
        side_work(("pool", g))
        sl = slice(g * gw, (g + 1) * gw)
        pooled = jnp.dot(band_ref[g], u_ref[:, sl], preferred_element_type=F32)
        head = jnp.dot(pband_ref[g], ptail_ref[:, sl], preferred_element_type=F32)
        start = jnp.dot(tril_ref[...], u_ref[head_rows, sl], preferred_element_type=F32)
        head = head + start * (first * fix_ref[:, sl])
        pooled = jnp.concatenate([pooled[:MAX_WINDOW] + head, pooled[MAX_WINDOW:]], axis=0)
        gate = gate_ref[:, sl].astype(F32)
        yp_ref[:, sl] = (pooled * _silu(gate)).astype(yp_ref.dtype)


def _ssd_chunk(c, z_ref, xbc_ref, dt_ref, shift_ref, pshift_ref, cw_ref, cb_ref, dtb_ref, alog_ref,
               drep_ref, nw_ref, ctail_ref, s_ref, y_ref, ys_ref, side_work):
    q = SSD_Q
    ssd_w = z_ref.shape[1]
    gn = SSD_GROUPS * SSD_STATE
    gwid = ssd_w // SSD_GROUPS
    pairs_per_group = gwid // LANES
    rows = slice(c * q, (c + 1) * q)

    def taps(frames, ntaps):
        nrow = frames.shape[0] // BF16_ROWS
        return jnp.concatenate(
            [frames[r * BF16_ROWS:(r + 1) * BF16_ROWS, :] * cw_ref[k]
             for k in range(ntaps) for r in range(nrow)], axis=0)

    before = ctail_ref[...] if c == 0 else xbc_ref[c * q - BF16_ROWS:c * q, :]
    acc = jnp.dot(shift_ref[...], taps(xbc_ref[rows, :], CONV_WIDTH), preferred_element_type=F32)
    head = jnp.dot(pshift_ref[...], taps(before, CONV_WIDTH - 1), preferred_element_type=F32)
    acc = jnp.concatenate([acc[:BF16_ROWS] + head, acc[BF16_ROWS:]], axis=0) + cb_ref[...]
    act = _silu(acc)

    dt = _softplus(dt_ref[rows, :] + dtb_ref[...])
    adt = dt * (-jnp.exp(alog_ref[...]))
    tril = (lax.broadcasted_iota(jnp.int32, (q, q), 0)
            >= lax.broadcasted_iota(jnp.int32, (q, q), 1))
    tri = jnp.where(tril, 1.0, 0.0).astype(BF16)
    a1 = adt.astype(BF16)
    r1 = adt - a1.astype(F32)
    a2 = r1.astype(BF16)
    a3 = (r1 - a2.astype(F32)).astype(BF16)
    acs2 = LOG2E * (jnp.dot(tri, a1, preferred_element_type=F32)
                    + jnp.dot(tri, a2, preferred_element_type=F32)
                    + jnp.dot(tri, a3, preferred_element_type=F32))
    bdt = acs2 - LOG2E * jnp.log(dt)
    wst = jnp.exp2(acs2[q - 1:q, :] - bdt)
    bdt_t = bdt.T

    lo_f = lax.broadcasted_iota(jnp.int32, (q, LANES), 1) < SSD_HEAD_DIM
    lo_b = jnp.where(lo_f, 1.0, 0.0).astype(BF16)
    hi_b = jnp.where(lo_f, 0.0, 1.0).astype(BF16)

    for g in range(SSD_GROUPS):
        side_work(("ssd", c, g))
        b_g = act[:, ssd_w + g * SSD_STATE: ssd_w + (g + 1) * SSD_STATE]
        c_g = act[:, ssd_w + gn + g * SSD_STATE: ssd_w + gn + (g + 1) * SSD_STATE]
        c_gb = c_g.astype(BF16)
        scores = lax.dot_general(c_gb, b_g.astype(BF16), NT_DIMS,
                                 preferred_element_type=F32)
        b_gtb = b_g.T.astype(BF16)
        pairs = range(g * pairs_per_group, (g + 1) * pairs_per_group)
        s_g = [s_ref[pair] for pair in pairs]
        y_off = jnp.dot(c_gb, jnp.concatenate([sp.astype(BF16) for sp in s_g], axis=1),
                        preferred_element_type=F32)
        xw_parts, cd_parts = [], []
        for j, pair in enumerate(pairs):
            cols = slice(j * LANES, (j + 1) * LANES)
            xp = act[:, pair * LANES:(pair + 1) * LANES]
            xpb = xp.astype(BF16)
            m_parts, acols, wcols = [], [], []
            for e in range(2):
                h = 2 * pair + e
                acol = jnp.broadcast_to(acs2[:, h:h + 1], (q, LANES))
                decay = jnp.where(tril, jnp.exp2(acol - bdt_t[h:h + 1, :]), 0.0)
                m_parts.append((scores * decay).astype(BF16))
                acols.append(acol)
                wcols.append(jnp.broadcast_to(wst[:, h:h + 1], (q, LANES)))
            ea = jnp.exp2(jnp.where(lo_f, acols[0], acols[1]))
            y_ref[c, :, pair * LANES:(pair + 1) * LANES] = (
                jnp.dot(jnp.concatenate(m_parts, axis=1),
                        jnp.concatenate([xpb * lo_b, xpb * hi_b], axis=0),
                        preferred_element_type=F32)
                + ea * y_off[:, cols])
            xw_parts.append((xp * jnp.where(lo_f, wcols[0], wcols[1])).astype(BF16))
            cd_parts.append(ea[q - 1:q, :])
        upd = jnp.dot(b_gtb, jnp.concatenate(xw_parts, axis=1),
                      preferred_element_type=F32)
        for j, pair in enumerate(pairs):
            s_ref[pair] = s_g[j] * cd_parts[j] + upd[:, j * LANES:(j + 1) * LANES]

    for g in range(SSD_GROUPS):
        sl = slice(g * gwid, (g + 1) * gwid)
        y = y_ref[c, :, sl] + drep_ref[:, sl] * act[:, sl]
        y = y * _silu(z_ref[rows, sl].astype(F32))
        ms = jnp.mean(y * y, axis=-1, keepdims=True)
        ys_ref[rows, sl] = (y * lax.rsqrt(ms + NORM_EPS) * nw_ref[:, sl]).astype(ys_ref.dtype)


def _mixer_out_kernel(u_ref, gate_ref, z_ref, xbc_ref, dt_ref, x_ref,
                      band_ref, pband_ref, tril_ref, fix_ref, shift_ref, pshift_ref,
                      cw_ref, cb_ref, dtb_ref, alog_ref, drep_ref, snw_ref,
                      wp_ref, ws_ref, pnw_ref,
                      o_ref,
                      ptail_ref, ctail_ref, s_ref, y_ref, yp_ref, ys_ref, ypc_ref, ysc_ref, out_ref,
                      *, tiles_per_seq):
    tt = u_ref.shape[0]
    i = pl.program_id(0)
    seq_start = (i % tiles_per_seq) == 0

    @pl.when(i == 0)
    def _():
        yp_ref[...] = jnp.zeros_like(yp_ref)
        ys_ref[...] = jnp.zeros_like(ys_ref)

    @pl.when(seq_start)
    def _():
        ptail_ref[...] = jnp.zeros_like(ptail_ref)
        ctail_ref[...] = jnp.zeros_like(ctail_ref)
        s_ref[...] = jnp.zeros_like(s_ref)

    ypc_ref[...] = yp_ref[...]
    ysc_ref[...] = ys_ref[...]

    nchunks = tt // SSD_Q
    n_slabs = len(SLAB_SLOTS)
    slab = o_ref.shape[1] // n_slabs
    assert slab % (2 * LANES) == 0 and slab * n_slabs == o_ref.shape[1]
    emitted = []

    def out_proj_slab(slot):
        if slot not in SLAB_SLOTS:
            return
        k = len(emitted)
        emitted.append(k)
        cols = slice(k * slab, (k + 1) * slab)
        out_ref[:, cols] = (jnp.dot(ypc_ref[...], wp_ref[:, cols], preferred_element_type=F32)
                            + jnp.dot(ysc_ref[...], ws_ref[:, cols], preferred_element_type=F32))

    first = jnp.where(seq_start, 1.0, 0.0).astype(F32)
    for c in range(nchunks):
        _ssd_chunk(c, z_ref, xbc_ref, dt_ref, shift_ref, pshift_ref, cw_ref, cb_ref, dtb_ref,
                   alog_ref, drep_ref, snw_ref, ctail_ref, s_ref, y_ref, ys_ref, out_proj_slab)
    _pool_part(u_ref, gate_ref, band_ref, pband_ref, tril_ref, fix_ref, ptail_ref, yp_ref, first,
               out_proj_slab)
    assert len(emitted) == n_slabs
    ptail_ref[...] = u_ref[tt - MAX_WINDOW:, :]
    ctail_ref[...] = xbc_ref[tt - BF16_ROWS:, :]

    out = out_ref[...]
    var = jnp.mean(out * out, axis=-1, keepdims=True)
    o_ref[...] = x_ref[...] + out * lax.rsqrt(var + NORM_EPS) * pnw_ref[...]


def _mixer_out(proj, dt_raw, x2d, conv_w16, conv_b, dt_bias, a_log, d_rep, ssd_nw,
               w_out, post_w, layer, batch, seq):
    m, d = x2d.shape
    ssd_w = ssd_nw.shape[-1]
    pw = ssd_w
    gw = pw // len(POOL_WINDOWS)
    conv_dim = conv_b.shape[-1]
    assert proj.shape[1] == 2 * pw + ssd_w + conv_dim and conv_dim * 2 == 3 * pw
    assert w_out.shape[1] == pw + ssd_w and MIX_TM % SSD_Q == 0
    nt = seq // MIX_TM
    ntiles = batch * nt
    npairs = ssd_w // LANES
    band, pband, tril, fix = _pool_constants(MIX_TM, gw)
    shift, pshift = _conv_constants(SSD_Q)
    row_blk = lambda col: (lambda i: (jnp.minimum(i, ntiles - 1), col))
    prev_blk = lambda i: (jnp.maximum(i - 1, 0), 0)
    resident = dict(pipeline_mode=pl.Buffered(1))
    return pl.pallas_call(
        functools.partial(_mixer_out_kernel, tiles_per_seq=nt),
        grid=(ntiles + 1,),
        in_specs=[
            pl.BlockSpec((MIX_TM, pw), row_blk(0)),
            pl.BlockSpec((MIX_TM, pw), row_blk(1)),
            pl.BlockSpec((MIX_TM, ssd_w), row_blk(2)),
            pl.BlockSpec((MIX_TM, conv_dim), row_blk(2)),
            pl.BlockSpec((MIX_TM, LANES), row_blk(0)),
            pl.BlockSpec((MIX_TM, d), prev_blk),
            _const_block(band), _const_block(pband), _const_block(tril), _const_block(fix),
            _const_block(shift), _const_block(pshift),
            _layer_block((CONV_WIDTH, BF16_ROWS, conv_dim), layer),
            _layer_block((1, conv_dim), layer),
            _layer_block((1, LANES), layer),
            _layer_block((1, LANES), layer),
            _layer_block((1, ssd_w), layer),
            _layer_block((1, ssd_w), layer),
            pl.BlockSpec((None, pw, d), lambda i: (layer, 0, 0), **resident),
            pl.BlockSpec((None, ssd_w, d), lambda i: (layer, 1, 0), **resident),
            _layer_block((1, d), layer),
        ],
        out_specs=pl.BlockSpec((MIX_TM, d), prev_blk),
        out_shape=jax.ShapeDtypeStruct((m, d), F32),
        scratch_shapes=[
            pltpu.VMEM((MAX_WINDOW, pw), BF16),
            pltpu.VMEM((BF16_ROWS, conv_dim), BF16),
            pltpu.VMEM((npairs, SSD_STATE, LANES), F32),
            pltpu.VMEM((MIX_TM // SSD_Q, SSD_Q, ssd_w), F32),
            pltpu.VMEM((MIX_TM, pw), BF16),
            pltpu.VMEM((MIX_TM, ssd_w), BF16),
            pltpu.VMEM((MIX_TM, pw), BF16),
            pltpu.VMEM((MIX_TM, ssd_w), BF16),
            pltpu.VMEM((MIX_TM, d), F32),
        ],
        compiler_params=pltpu.CompilerParams(
            dimension_semantics=("arbitrary",),
            vmem_limit_bytes=VMEM_LIMIT),
        name="mixer_out",
    )(proj, proj, proj, proj, dt_raw, x2d, band, pband, tril, fix, shift, pshift,
      conv_w16, conv_b, dt_bias, a_log, d_rep, ssd_nw, w_out, w_out, post_w)


def kernel(x, pre_norm_w, w_in, pool_mix_w, pool_scale, conv_w, conv_b, dt_bias, a_log, d_skip,
           ssd_norm_w, w_out, post_norm_w):
    batch, seq, d_model = x.shape
    depth = w_in.shape[0]
    pool_w = pool_scale.shape[1]
    ssd_w = ssd_norm_w.shape[1]
    heads = dt_bias.shape[1]
    conv_dim = conv_w.shape[2]
    n_main = w_in.shape[2] - heads
    assert ssd_w == heads * SSD_HEAD_DIM and heads <= LANES and pool_w == ssd_w
    assert n_main == 2 * pool_w + ssd_w + conv_dim and n_main % IN_TN == 0
    assert seq % MIX_TM == 0 and (batch * seq) % IN_TM == 0

    head_pad = ((0, 0), (0, LANES - heads))
    w_in_t = jnp.swapaxes(w_in, 1, 2)
    w_dt_t = jnp.pad(w_in_t[:, n_main:, :], ((0, 0), (0, LANES - heads), (0, 0))).astype(BF16)
    w_out_b = w_out.astype(BF16)
    conv_w16 = jnp.broadcast_to(conv_w.astype(BF16)[:, :, None, :],
                                (depth, CONV_WIDTH, BF16_ROWS, conv_dim))
    rows = lambda p: p.reshape(depth, 1, -1)
    pre_w, post_w = rows(pre_norm_w), rows(post_norm_w)
    conv_b3, ssd_nw = rows(conv_b), rows(ssd_norm_w)
    w_main_t = _fold_pool_map(w_in_t, w_in_t.astype(BF16), pool_mix_w, rows(pool_scale))
    dt_bias3, a_log3 = rows(jnp.pad(dt_bias, head_pad)), rows(jnp.pad(a_log, head_pad))
    d_rep = rows(jnp.repeat(d_skip, SSD_HEAD_DIM, axis=1))

    x2d = x.reshape(batch * seq, d_model)
    for layer in range(depth):
        proj, dt_raw = _in_proj(x2d, pre_w, w_main_t, w_dt_t, layer, n_main)
        x2d = _mixer_out(proj, dt_raw, x2d, conv_w16, conv_b3, dt_bias3, a_log3,
                         d_rep, ssd_nw, w_out_b, post_w, layer, batch, seq)
    return x2d.reshape(batch, seq, d_model)
```

```python
import functools

import jax
import jax.numpy as jnp
import numpy as np
from jax import lax
from jax.experimental import pallas as pl
from jax.experimental.pallas import tpu as pltpu

F32 = jnp.float32
BF16 = jnp.bfloat16

POOL_WINDOWS = (2, 4, 8, 16)
MAX_WINDOW = max(POOL_WINDOWS)
SSD_HEAD_DIM = 64
SSD_STATE = 128
SSD_GROUPS = 4
CONV_WIDTH = 4
NORM_EPS = 1e-6
LOG2E = 1.4426950408889634

LANES = 128
BF16_ROWS = 16
VMEM_LIMIT = 56 * 1024 * 1024

IN_TM = 1024
IN_TN = 2304
IN_CH = 256
MIX_TM = 256
SSD_Q = 128

SLAB_SLOTS = (("ssd", 0, 0), ("ssd", 0, 1), ("ssd", 0, 2), ("ssd", 0, 3),
              ("ssd", 1, 0), ("ssd", 1, 1), ("ssd", 1, 2), ("pool", 0))

NT_DIMS = (((1,), (1,)), ((), ()))


def _silu(v):
    hv = 0.5 * v
    return hv + hv * jnp.tanh(hv)


def _softplus(v):
    return jnp.maximum(v, 0.0) + jnp.log(1.0 + jnp.exp(-jnp.abs(v)))


def _layer_block(shape, layer, *tail):
    if tail:
        return pl.BlockSpec((None,) + shape, lambda *g: (layer,) + tuple(tail[0](*g)))
    return pl.BlockSpec((None,) + shape, lambda *g: (layer,) + (0,) * len(shape))


def _const_block(arr):
    return pl.BlockSpec(arr.shape, lambda *g: (0,) * arr.ndim)


def _in_proj_kernel(x_ref, nw_ref, w_ref, wdt_ref, o_ref, dt_ref, h_ref):
    j = pl.program_id(1)

    @pl.when(j == 0)
    def _():
        for r in range(IN_TM // IN_CH):
            rows = slice(r * IN_CH, (r + 1) * IN_CH)
            x = x_ref[rows, :]
            var = jnp.mean(x * x, axis=-1, keepdims=True)
            h = (x * lax.rsqrt(var + NORM_EPS) * nw_ref[...]).astype(BF16)
            h_ref[rows, :] = h
            dt_ref[rows, :] = lax.dot_general(h, wdt_ref[...], NT_DIMS, preferred_element_type=F32)
            o_ref[rows, :] = lax.dot_general(h, w_ref[...], NT_DIMS,
                                             preferred_element_type=F32).astype(o_ref.dtype)

    @pl.when(j != 0)
    def _():
        o_ref[...] = lax.dot_general(h_ref[...], w_ref[...], NT_DIMS,
                                     preferred_element_type=F32).astype(o_ref.dtype)


def _in_proj(x2d, norm_w, w_in_t, w_dt_t, layer, n_main):
    m, d = x2d.shape
    return pl.pallas_call(
        _in_proj_kernel,
        grid=(m // IN_TM, n_main // IN_TN),
        in_specs=[
            pl.BlockSpec((IN_TM, d), lambda i, j: (i, 0)),
            _layer_block((1, d), layer),
            _layer_block((IN_TN, d), layer, lambda i, j: (j, 0)),
            _layer_block((LANES, d), layer),
        ],
        out_specs=[
            pl.BlockSpec((IN_TM, IN_TN), lambda i, j: (i, j)),
            pl.BlockSpec((IN_TM, LANES), lambda i, j: (i, 0)),
        ],
        out_shape=[
            jax.ShapeDtypeStruct((m, n_main), BF16),
            jax.ShapeDtypeStruct((m, LANES), F32),
        ],
        scratch_shapes=[pltpu.VMEM((IN_TM, d), BF16)],
        compiler_params=pltpu.CompilerParams(
            dimension_semantics=("arbitrary", "arbitrary"),
            vmem_limit_bytes=VMEM_LIMIT),
        name="in_proj",
    )(x2d, norm_w, w_in_t, w_dt_t)


def _prep_kernel(w_ref, mix_ref, scale_ref, o_ref, *, fold_blocks):
    blk = pl.program_id(1)

    @pl.when(blk < fold_blocks)
    def _():
        wm = (mix_ref[...] * scale_ref[...]).astype(BF16)
        o_ref[...] = lax.dot_general(wm, w_ref[...].astype(BF16), (((0,), (0,)), ((), ())),
                                     preferred_element_type=F32).astype(o_ref.dtype)

    @pl.when(blk >= fold_blocks)
    def _():
        o_ref[...] = w_ref[...].astype(o_ref.dtype)


def _prep_w_in(w_in_t, mix_w, scale, n_main):
    depth, _, d = w_in_t.shape
    _, ngroups, gw, _ = mix_w.shape
    assert n_main % gw == 0
    group = lambda blk: jnp.minimum(blk, ngroups - 1)
    rows_blk = pl.BlockSpec((None, gw, d), lambda l, b: (l, b, 0))
    return pl.pallas_call(
        functools.partial(_prep_kernel, fold_blocks=ngroups),
        grid=(depth, n_main // gw),
        in_specs=[
            rows_blk,
            pl.BlockSpec((None, None, gw, gw), lambda l, b: (l, group(b), 0, 0)),
            pl.BlockSpec((None, 1, gw), lambda l, b: (l, 0, group(b))),
        ],
        out_specs=rows_blk,
        out_shape=jax.ShapeDtypeStruct((depth, n_main, d), BF16),
        compiler_params=pltpu.CompilerParams(
            dimension_semantics=("arbitrary", "arbitrary"),
            vmem_limit_bytes=VMEM_LIMIT),
        name="prep_w_in",
    )(w_in_t, mix_w, scale)


def _pool_constants(tt, gw):
    i = np.arange(tt)[:, None]
    j = np.arange(tt)[None, :]
    pi = np.arange(MAX_WINDOW)[:, None]
    pj = np.arange(MAX_WINDOW)[None, :]
    band, pband, fix = [], [], []
    for w in POOL_WINDOWS:
        inside = (j <= i) & (j > i - w)
        band.append(inside / w - (i == j))
        pband.append((pj > pi + (MAX_WINDOW - w)) / w)
        fix.append(np.repeat(1.0 / np.minimum(pi + 1, w) - 1.0 / w, gw, axis=1))
    return (jnp.asarray(np.stack(band), dtype=BF16), jnp.asarray(np.stack(pband), dtype=BF16),
            jnp.asarray(pj <= pi, dtype=BF16), jnp.asarray(np.concatenate(fix, axis=1), dtype=F32))


def _conv_constants(q):
    t = np.arange(q)[:, None]
    s = np.arange(q)[None, :]
    pt = np.arange(BF16_ROWS)[:, None]
    pj = np.arange(BF16_ROWS)[None, :]
    shift, pshift = [], []
    for k in range(CONV_WIDTH):
        d = CONV_WIDTH - 1 - k
        shift.append(s == t - d)
        if d:
            pshift.append((pj == BF16_ROWS - d + pt) & (pt < d))
    return (jnp.asarray(np.concatenate(shift, axis=1), dtype=BF16),
            jnp.asarray(np.concatenate(pshift, axis=1), dtype=BF16))


def _pool_part(u_ref, gate_ref, band_ref, pband_ref, tril_ref, fix_ref, ptail_ref, yp_ref, first,
               side_work):
    gw = u_ref.shape[1] // len(POOL_WINDOWS)
    head_rows = slice(0, MAX_WINDOW)
    for g in range(len(POOL_WINDOWS)):
        side_work(("pool", g))
        sl = slice(g * gw, (g + 1) * gw)
        pooled = jnp.dot(band_ref[g], u_ref[:, sl], preferred_element_type=F32)
        head = jnp.dot(pband_ref[g], ptail_ref[:, sl], preferred_element_type=F32)
        start = jnp.dot(tril_ref[...], u_ref[head_rows, sl], preferred_element_type=F32)
        head = head + start * (first * fix_ref[:, sl])
        pooled = jnp.concatenate([pooled[:MAX_WINDOW] + head, pooled[MAX_WINDOW:]], axis=0)
        gate = gate_ref[:, sl].astype(F32)
        yp_ref[:, sl] = (pooled * _silu(gate)).astype(yp_ref.dtype)


def _ssd_chunk(c, z_ref, xbc_ref, dt_ref, shift_ref, pshift_ref, cw_ref, cb_ref, dtb_ref, alog_ref,
               drep_ref, nw_ref, ctail_ref, s_ref, y_ref, ys_ref, side_work):
    q = SSD_Q
    ssd_w = z_ref.shape[1]
    gn = SSD_GROUPS * SSD_STATE
    gwid = ssd_w // SSD_GROUPS
    pairs_per_group = gwid // LANES
    rows = slice(c * q, (c + 1) * q)

    def taps(frames, ntaps):
        nrow = frames.shape[0] // BF16_ROWS
        return jnp.concatenate(
            [frames[r * BF16_ROWS:(r + 1) * BF16_ROWS, :] * cw_ref[k]
             for k in range(ntaps) for r in range(nrow)], axis=0)

    before = ctail_ref[...] if c == 0 else xbc_ref[c * q - BF16_ROWS:c * q, :]
    acc = jnp.dot(shift_ref[...], taps(xbc_ref[rows, :], CONV_WIDTH), preferred_element_type=F32)
    head = jnp.dot(pshift_ref[...], taps(before, CONV_WIDTH - 1), preferred_element_type=F32)
    acc = jnp.concatenate([acc[:BF16_ROWS] + head, acc[BF16_ROWS:]], axis=0) + cb_ref[...]
    act = _silu(acc)

    dt = _softplus(dt_ref[rows, :] + dtb_ref[...])
    adt = dt * (-jnp.exp(alog_ref[...]))
    tril = (lax.broadcasted_iota(jnp.int32, (q, q), 0)
            >= lax.broadcasted_iota(jnp.int32, (q, q), 1))
    tri = jnp.where(tril, 1.0, 0.0).astype(BF16)
    a1 = adt.astype(BF16)
    r1 = adt - a1.astype(F32)
    a2 = r1.astype(BF16)
    a3 = (r1 - a2.astype(F32)).astype(BF16)
    acs2 = LOG2E * (jnp.dot(tri, a1, preferred_element_type=F32)
                    + jnp.dot(tri, a2, preferred_element_type=F32)
                    + jnp.dot(tri, a3, preferred_element_type=F32))
    bdt = acs2 - LOG2E * jnp.log(dt)
    wst = jnp.exp2(acs2[q - 1:q, :] - bdt)
    bdt_t = bdt.T

    lo_f = lax.broadcasted_iota(jnp.int32, (q, LANES), 1) < SSD_HEAD_DIM
    lo_b = jnp.where(lo_f, 1.0, 0.0).astype(BF16)
    hi_b = jnp.where(lo_f, 0.0, 1.0).astype(BF16)

    for g in range(SSD_GROUPS):
        side_work(("ssd", c, g))
        b_g = act[:, ssd_w + g * SSD_STATE: ssd_w + (g + 1) * SSD_STATE]
        c_g = act[:, ssd_w + gn + g * SSD_STATE: ssd_w + gn + (g + 1) * SSD_STATE]
        c_gb = c_g.astype(BF16)
        scores = lax.dot_general(c_gb, b_g.astype(BF16), NT_DIMS,
                                 preferred_element_type=F32)
        b_gtb = b_g.T.astype(BF16)
        pairs = range(g * pairs_per_group, (g + 1) * pairs_per_group)
        s_g = [s_ref[pair] for pair in pairs]
        y_off = jnp.dot(c_gb, jnp.concatenate([sp.astype(BF16) for sp in s_g], axis=1),
                        preferred_element_type=F32)
        xw_parts, cd_parts = [], []
        for j, pair in enumerate(pairs):
            cols = slice(j * LANES, (j + 1) * LANES)
            xp = act[:, pair * LANES:(pair + 1) * LANES]
            xpb = xp.astype(BF16)
            m_parts, acols, wcols = [], [], []
            for e in range(2):
                h = 2 * pair + e
                acol = jnp.broadcast_to(acs2[:, h:h + 1], (q, LANES))
                decay = jnp.where(tril, jnp.exp2(acol - bdt_t[h:h + 1, :]), 0.0)
                m_parts.append((scores * decay).astype(BF16))
                acols.append(acol)
                wcols.append(jnp.broadcast_to(wst[:, h:h + 1], (q, LANES)))
            ea = jnp.exp2(jnp.where(lo_f, acols[0], acols[1]))
            y_ref[c, :, pair * LANES:(pair + 1) * LANES] = (
                jnp.dot(jnp.concatenate(m_parts, axis=1),
                        jnp.concatenate([xpb * lo_b, xpb * hi_b], axis=0),
                        preferred_element_type=F32)
                + ea * y_off[:, cols])
            xw_parts.append((xp * jnp.where(lo_f, wcols[0], wcols[1])).astype(BF16))
            cd_parts.append(ea[q - 1:q, :])
        upd = jnp.dot(b_gtb, jnp.concatenate(xw_parts, axis=1),
                      preferred_element_type=F32)
        for j, pair in enumerate(pairs):
            s_ref[pair] = s_g[j] * cd_parts[j] + upd[:, j * LANES:(j + 1) * LANES]

    for g in range(SSD_GROUPS):
        sl = slice(g * gwid, (g + 1) * gwid)
        y = y_ref[c, :, sl] + drep_ref[:, sl] * act[:, sl]
        y = y * _silu(z_ref[rows, sl].astype(F32))
        ms = jnp.mean(y * y, axis=-1, keepdims=True)
        ys_ref[rows, sl] = (y * lax.rsqrt(ms + NORM_EPS) * nw_ref[:, sl]).astype(ys_ref.dtype)


def _mixer_out_kernel(u_ref, gate_ref, z_ref, xbc_ref, dt_ref, x_ref,
                      band_ref, pband_ref, tril_ref, fix_ref, shift_ref, pshift_ref,
                      cw_ref, cb_ref, dtb_ref, alog_ref, drep_ref, snw_ref,
                      wp_ref, ws_ref, pnw_ref,
                      o_ref,
                      ptail_ref, ctail_ref, s_ref, y_ref, yp_ref, ys_ref, ypc_ref, ysc_ref, out_ref,
                      *, tiles_per_seq):
    tt = u_ref.shape[0]
    i = pl.program_id(0)
    seq_start = (i % tiles_per_seq) == 0

    @pl.when(i == 0)
    def _():
        yp_ref[...] = jnp.zeros_like(yp_ref)
        ys_ref[...] = jnp.zeros_like(ys_ref)

    @pl.when(seq_start)
    def _():
        ptail_ref[...] = jnp.zeros_like(ptail_ref)
        ctail_ref[...] = jnp.zeros_like(ctail_ref)
        s_ref[...] = jnp.zeros_like(s_ref)

    ypc_ref[...] = yp_ref[...]
    ysc_ref[...] = ys_ref[...]

    nchunks = tt // SSD_Q
    n_slabs = len(SLAB_SLOTS)
    slab = o_ref.shape[1] // n_slabs
    assert slab % (2 * LANES) == 0 and slab * n_slabs == o_ref.shape[1]
    emitted = []

    def out_proj_slab(slot):
        if slot not in SLAB_SLOTS:
            return
        k = len(emitted)
        emitted.append(k)
        cols = slice(k * slab, (k + 1) * slab)
        out_ref[:, cols] = (jnp.dot(ypc_ref[...], wp_ref[:, cols], preferred_element_type=F32)
                            + jnp.dot(ysc_ref[...], ws_ref[:, cols], preferred_element_type=F32))

    first = jnp.where(seq_start, 1.0, 0.0).astype(F32)
    for c in range(nchunks):
        _ssd_chunk(c, z_ref, xbc_ref, dt_ref, shift_ref, pshift_ref, cw_ref, cb_ref, dtb_ref,
                   alog_ref, drep_ref, snw_ref, ctail_ref, s_ref, y_ref, ys_ref, out_proj_slab)
    _pool_part(u_ref, gate_ref, band_ref, pband_ref, tril_ref, fix_ref, ptail_ref, yp_ref, first,
               out_proj_slab)
    assert len(emitted) == n_slabs
    ptail_ref[...] = u_ref[tt - MAX_WINDOW:, :]
    ctail_ref[...] = xbc_ref[tt - BF16_ROWS:, :]

    out = out_ref[...]
    var = jnp.mean(out * out, axis=-1, keepdims=True)
    o_ref[...] = x_ref[...] + out * lax.rsqrt(var + NORM_EPS) * pnw_ref[...]


def _mixer_out(proj, dt_raw, x2d, conv_w16, conv_b, dt_bias, a_log, d_rep, ssd_nw,
               w_out, post_w, layer, batch, seq):
    m, d = x2d.shape
    ssd_w = ssd_nw.shape[-1]
    pw = ssd_w
    gw = pw // len(POOL_WINDOWS)
    conv_dim = conv_b.shape[-1]
    assert proj.shape[1] == 2 * pw + ssd_w + conv_dim and conv_dim * 2 == 3 * pw
    assert w_out.shape[1] == pw + ssd_w and MIX_TM % SSD_Q == 0
    nt = seq // MIX_TM
    ntiles = batch * nt
    npairs = ssd_w // LANES
    band, pband, tril, fix = _pool_constants(MIX_TM, gw)
    shift, pshift = _conv_constants(SSD_Q)
    row_blk = lambda col: (lambda i: (jnp.minimum(i, ntiles - 1), col))
    prev_blk = lambda i: (jnp.maximum(i - 1, 0), 0)
    resident = dict(pipeline_mode=pl.Buffered(1))
    return pl.pallas_call(
        functools.partial(_mixer_out_kernel, tiles_per_seq=nt),
        grid=(ntiles + 1,),
        in_specs=[
            pl.BlockSpec((MIX_TM, pw), row_blk(0)),
            pl.BlockSpec((MIX_TM, pw), row_blk(1)),
            pl.BlockSpec((MIX_TM, ssd_w), row_blk(2)),
            pl.BlockSpec((MIX_TM, conv_dim), row_blk(2)),
            pl.BlockSpec((MIX_TM, LANES), row_blk(0)),
            pl.BlockSpec((MIX_TM, d), prev_blk),
            _const_block(band), _const_block(pband), _const_block(tril), _const_block(fix),
            _const_block(shift), _const_block(pshift),
            _layer_block((CONV_WIDTH, BF16_ROWS, conv_dim), layer),
            _layer_block((1, conv_dim), layer),
            _layer_block((1, LANES), layer),
            _layer_block((1, LANES), layer),
            _layer_block((1, ssd_w), layer),
            _layer_block((1, ssd_w), layer),
            pl.BlockSpec((None, pw, d), lambda i: (layer, 0, 0), **resident),
            pl.BlockSpec((None, ssd_w, d), lambda i: (layer, 1, 0), **resident),
            _layer_block((1, d), layer),
        ],
        out_specs=pl.BlockSpec((MIX_TM, d), prev_blk),
        out_shape=jax.ShapeDtypeStruct((m, d), F32),
        scratch_shapes=[
            pltpu.VMEM((MAX_WINDOW, pw), BF16),
            pltpu.VMEM((BF16_ROWS, conv_dim), BF16),
            pltpu.VMEM((npairs, SSD_STATE, LANES), F32),
            pltpu.VMEM((MIX_TM // SSD_Q, SSD_Q, ssd_w), F32),
            pltpu.VMEM((MIX_TM, pw), BF16),
            pltpu.VMEM((MIX_TM, ssd_w), BF16),
            pltpu.VMEM((MIX_TM, pw), BF16),
            pltpu.VMEM((MIX_TM, ssd_w), BF16),
            pltpu.VMEM((MIX_TM, d), F32),
        ],
        compiler_params=pltpu.CompilerParams(
            dimension_semantics=("arbitrary",),
            vmem_limit_bytes=VMEM_LIMIT),
        name="mixer_out",
    )(proj, proj, proj, proj, dt_raw, x2d, band, pband, tril, fix, shift, pshift,
      conv_w16, conv_b, dt_bias, a_log, d_rep, ssd_nw, w_out, w_out, post_w)


def kernel(x, pre_norm_w, w_in, pool_mix_w, pool_scale, conv_w, conv_b, dt_bias, a_log, d_skip,
           ssd_norm_w, w_out, post_norm_w):
    batch, seq, d_model = x.shape
    depth = w_in.shape[0]
    pool_w = pool_scale.shape[1]
    ssd_w = ssd_norm_w.shape[1]
    heads = dt_bias.shape[1]
    conv_dim = conv_w.shape[2]
    n_main = w_in.shape[2] - heads
    assert ssd_w == heads * SSD_HEAD_DIM and heads <= LANES and pool_w == ssd_w
    assert n_main == 2 * pool_w + ssd_w + conv_dim and n_main % IN_TN == 0
    assert seq % MIX_TM == 0 and (batch * seq) % IN_TM == 0

    head_pad = ((0, 0), (0, LANES - heads))
    w_in_t = jnp.swapaxes(w_in, 1, 2)
    w_dt_t = jnp.pad(w_in_t[:, n_main:, :], ((0, 0), (0, LANES - heads), (0, 0))).astype(BF16)
    w_out_b = w_out.astype(BF16)
    conv_w16 = jnp.broadcast_to(conv_w.astype(BF16)[:, :, None, :],
                                (depth, CONV_WIDTH, BF16_ROWS, conv_dim))
    rows = lambda p: p.reshape(depth, 1, -1)
    pre_w, post_w = rows(pre_norm_w), rows(post_norm_w)
    conv_b3, ssd_nw = rows(conv_b), rows(ssd_norm_w)
    w_main_t = _prep_w_in(w_in_t, pool_mix_w, rows(pool_scale), n_main)
    dt_bias3, a_log3 = rows(jnp.pad(dt_bias, head_pad)), rows(jnp.pad(a_log, head_pad))
    d_rep = rows(jnp.repeat(d_skip, SSD_HEAD_DIM, axis=1))

    x2d = x.reshape(batch * seq, d_model)
    for layer in range(depth):
        proj, dt_raw = _in_proj(x2d, pre_w, w_main_t, w_dt_t, layer, n_main)
        x2d = _mixer_out(proj, dt_raw, x2d, conv_w16, conv_b3, dt_bias3, a_log3,
                         d_rep, ssd_nw, w_out_b, post_w, layer, batch, seq)
    return x2d.reshape(batch, seq, d_model)
```

```python
import functools

import jax
import jax.numpy as jnp
import numpy as np
from jax import lax
from jax.experimental import pallas as pl
from jax.experimental.pallas import tpu as pltpu

F32 = jnp.float32
BF16 = jnp.bfloat16

POOL_WINDOWS = (2, 4, 8, 16)
MAX_WINDOW = max(POOL_WINDOWS)
SSD_HEAD_DIM = 64
SSD_STATE = 128
SSD_GROUPS = 4
CONV_WIDTH = 4
NORM_EPS = 1e-6
LOG2E = 1.4426950408889634

LANES = 128
BF16_ROWS = 16
VMEM_LIMIT = 56 * 1024 * 1024

IN_TM = 1024
IN_TN = 2304
IN_CH = 256
MIX_TM = 256
SSD_Q = 128

SLAB_SLOTS = (("ssd", 0, 0), ("ssd", 0, 1), ("ssd", 0, 2), ("ssd", 0, 3),
              ("ssd", 1, 0), ("ssd", 1, 1), ("ssd", 1, 2), ("pool", 0))

NT_DIMS = (((1,), (1,)), ((), ()))


def _silu(v):
    hv = 0.5 * v
    return hv + hv * jnp.tanh(hv)


def _softplus(v):
    return jnp.maximum(v, 0.0) + jnp.log(1.0 + jnp.exp(-jnp.abs(v)))


def _layer_block(shape, layer, *tail):
    if tail:
        return pl.BlockSpec((None,) + shape, lambda *g: (layer,) + tuple(tail[0](*g)))
    return pl.BlockSpec((None,) + shape, lambda *g: (layer,) + (0,) * len(shape))


def _const_block(arr):
    return pl.BlockSpec(arr.shape, lambda *g: (0,) * arr.ndim)


def _in_proj_kernel(x_ref, nw_ref, w_ref, wdt_ref, o_ref, dt_ref, h_ref):
    j = pl.program_id(1)

    @pl.when(j == 0)
    def _():
        for r in range(IN_TM // IN_CH):
            rows = slice(r * IN_CH, (r + 1) * IN_CH)
            x = x_ref[rows, :]
            var = jnp.mean(x * x, axis=-1, keepdims=True)
            h = (x * lax.rsqrt(var + NORM_EPS) * nw_ref[...]).astype(BF16)
            h_ref[rows, :] = h
            dt_ref[rows, :] = lax.dot_general(h, wdt_ref[...], NT_DIMS, preferred_element_type=F32)
            o_ref[rows, :] = lax.dot_general(h, w_ref[...], NT_DIMS,
                                             preferred_element_type=F32).astype(o_ref.dtype)

    @pl.when(j != 0)
    def _():
        o_ref[...] = lax.dot_general(h_ref[...], w_ref[...], NT_DIMS,
                                     preferred_element_type=F32).astype(o_ref.dtype)


def _in_proj(x2d, norm_w, w_in_t, w_dt_t, layer, n_main):
    m, d = x2d.shape
    return pl.pallas_call(
        _in_proj_kernel,
        grid=(m // IN_TM, n_main // IN_TN),
        in_specs=[
            pl.BlockSpec((IN_TM, d), lambda i, j: (i, 0)),
            _layer_block((1, d), layer),
            _layer_block((IN_TN, d), layer, lambda i, j: (j, 0)),
            _layer_block((LANES, d), layer),
        ],
        out_specs=[
            pl.BlockSpec((IN_TM, IN_TN), lambda i, j: (i, j)),
            pl.BlockSpec((IN_TM, LANES), lambda i, j: (i, 0)),
        ],
        out_shape=[
            jax.ShapeDtypeStruct((m, n_main), BF16),
            jax.ShapeDtypeStruct((m, LANES), F32),
        ],
        scratch_shapes=[pltpu.VMEM((IN_TM, d), BF16)],
        compiler_params=pltpu.CompilerParams(
            dimension_semantics=("arbitrary", "arbitrary"),
            vmem_limit_bytes=VMEM_LIMIT),
        name="in_proj",
    )(x2d, norm_w, w_in_t, w_dt_t)


def _prep_kernel(w_ref, mix_ref, scale_ref, o_ref, *, fold_blocks):
    blk = pl.program_id(1)

    @pl.when(blk < fold_blocks)
    def _():
        wm = (mix_ref[...] * scale_ref[...]).astype(BF16)
        o_ref[...] = lax.dot_general(wm, w_ref[...].astype(BF16), (((0,), (0,)), ((), ())),
                                     preferred_element_type=F32).astype(o_ref.dtype)

    @pl.when(blk >= fold_blocks)
    def _():
        o_ref[...] = w_ref[...].astype(o_ref.dtype)


def _prep_w_in(w_in_t, mix_w, scale, n_main):
    depth, _, d = w_in_t.shape
    _, ngroups, gw, _ = mix_w.shape
    assert n_main % gw == 0
    group = lambda blk: jnp.minimum(blk, ngroups - 1)
    rows_blk = pl.BlockSpec((None, gw, d), lambda l, b: (l, b, 0))
    return pl.pallas_call(
        functools.partial(_prep_kernel, fold_blocks=ngroups),
        grid=(depth, n_main // gw),
        in_specs=[
            rows_blk,
            pl.BlockSpec((None, None, gw, gw), lambda l, b: (l, group(b), 0, 0)),
            pl.BlockSpec((None, 1, gw), lambda l, b: (l, 0, group(b))),
        ],
        out_specs=rows_blk,
        out_shape=jax.ShapeDtypeStruct((depth, n_main, d), BF16),
        compiler_params=pltpu.CompilerParams(
            dimension_semantics=("arbitrary", "arbitrary"),
            vmem_limit_bytes=VMEM_LIMIT),
        name="prep_w_in",
    )(w_in_t, mix_w, scale)


def _pool_constants(tt, gw):
    i = np.arange(tt)[:, None]
    j = np.arange(tt)[None, :]
    pi = np.arange(MAX_WINDOW)[:, None]
    pj = np.arange(MAX_WINDOW)[None, :]
    band, pband, fix = [], [], []
    for w in POOL_WINDOWS:
        inside = (j <= i) & (j > i - w)
        band.append(inside / w - (i == j))
        pband.append((pj > pi + (MAX_WINDOW - w)) / w)
        fix.append(np.repeat(1.0 / np.minimum(pi + 1, w) - 1.0 / w, gw, axis=1))
    return (jnp.asarray(np.stack(band), dtype=BF16), jnp.asarray(np.stack(pband), dtype=BF16),
            jnp.asarray(pj <= pi, dtype=BF16), jnp.asarray(np.concatenate(fix, axis=1), dtype=F32))


def _conv_constants(q):
    t = np.arange(q)[:, None]
    s = np.arange(q)[None, :]
    pt = np.arange(BF16_ROWS)[:, None]
    pj = np.arange(BF16_ROWS)[None, :]
    shift, pshift = [], []
    for k in range(CONV_WIDTH):
        d = CONV_WIDTH - 1 - k
        shift.append(s == t - d)
        if d:
            pshift.append((pj == BF16_ROWS - d + pt) & (pt < d))
    return (jnp.asarray(np.concatenate(shift, axis=1), dtype=BF16),
            jnp.asarray(np.concatenate(pshift, axis=1), dtype=BF16))


def _pool_part(u_ref, gate_ref, band_ref, pband_ref, tril_ref, fix_ref, ptail_ref, yp_ref, first,
               side_work):
    gw = u_ref.shape[1] // len(POOL_WINDOWS)
    head_rows = slice(0, MAX_WINDOW)
    for g in range(len(POOL_WINDOWS)):
        side_work(("pool", g))
        sl = slice(g * gw, (g + 1) * gw)
        pooled = jnp.dot(band_ref[g], u_ref[:, sl], preferred_element_type=F32)
        head = jnp.dot(pband_ref[g], ptail_ref[:, sl], preferred_element_type=F32)
        start = jnp.dot(tril_ref[...], u_ref[head_rows, sl], preferred_element_type=F32)
        head = head + start * (first * fix_ref[:, sl])
        pooled = jnp.concatenate([pooled[:MAX_WINDOW] + head, pooled[MAX_WINDOW:]], axis=0)
        gate = gate_ref[:, sl].astype(F32)
        yp_ref[:, sl] = (pooled * _silu(gate)).astype(yp_ref.dtype)


def _ssd_chunk(c, z_ref, xbc_ref, dt_ref, shift_ref, pshift_ref, cw_ref, cb_ref, dtb_ref, alog_ref,
               drep_ref, nw_ref, ctail_ref, s_ref, y_ref, ys_ref, side_work):
    q = SSD_Q
    ssd_w = z_ref.shape[1]
    gn = SSD_GROUPS * SSD_STATE
    gwid = ssd_w // SSD_GROUPS
    pairs_per_group = gwid // LANES
    rows = slice(c * q, (c + 1) * q)

    def taps(frames, ntaps):
        nrow = frames.shape[0] // BF16_ROWS
        return jnp.concatenate(
            [frames[r * BF16_ROWS:(r + 1) * BF16_ROWS, :] * cw_ref[k]
             for k in range(ntaps) for r in range(nrow)], axis=0)

    before = ctail_ref[...] if c == 0 else xbc_ref[c * q - BF16_ROWS:c * q, :]
    acc = jnp.dot(shift_ref[...], taps(xbc_ref[rows, :], CONV_WIDTH), preferred_element_type=F32)
    head = jnp.dot(pshift_ref[...], taps(before, CONV_WIDTH - 1), preferred_element_type=F32)
    acc = jnp.concatenate([acc[:BF16_ROWS] + head, acc[BF16_ROWS:]], axis=0) + cb_ref[...]
    act = _silu(acc)

    dt = _softplus(dt_ref[rows, :] + dtb_ref[...])
    adt = dt * (-jnp.exp(alog_ref[...]))
    tril = (lax.broadcasted_iota(jnp.int32, (q, q), 0)
            >= lax.broadcasted_iota(jnp.int32, (q, q), 1))
    tri = jnp.where(tril, 1.0, 0.0).astype(BF16)
    a1 = adt.astype(BF16)
    r1 = adt - a1.astype(F32)
    a2 = r1.astype(BF16)
    a3 = (r1 - a2.astype(F32)).astype(BF16)
    acs2 = LOG2E * (jnp.dot(tri, a1, preferred_element_type=F32)
                    + jnp.dot(tri, a2, preferred_element_type=F32)
                    + jnp.dot(tri, a3, preferred_element_type=F32))
    bdt = acs2 - LOG2E * jnp.log(dt)
    wst = jnp.exp2(acs2[q - 1:q, :] - bdt)
    bdt_t = bdt.T

    yield
    lo_f =lax.broadcasted_iota(jnp.int32, (q, LANES), 1) < SSD_HEAD_DIM
    lo_b = jnp.where(lo_f, 1.0, 0.0).astype(BF16)
    hi_b = jnp.where(lo_f, 0.0, 1.0).astype(BF16)

    for g in range(SSD_GROUPS):
        side_work(("ssd", c, g))
        b_g = act[:, ssd_w + g * SSD_STATE: ssd_w + (g + 1) * SSD_STATE]
        c_g = act[:, ssd_w + gn + g * SSD_STATE: ssd_w + gn + (g + 1) * SSD_STATE]
        c_gb = c_g.astype(BF16)
        scores = lax.dot_general(c_gb, b_g.astype(BF16), NT_DIMS,
                                 preferred_element_type=F32)
        b_gtb = b_g.T.astype(BF16)
        pairs = range(g * pairs_per_group, (g + 1) * pairs_per_group)
        s_g = [s_ref[pair] for pair in pairs]
        y_off = jnp.dot(c_gb, jnp.concatenate([sp.astype(BF16) for sp in s_g], axis=1),
                        preferred_element_type=F32)
        xw_parts, cd_parts = [], []
        for j, pair in enumerate(pairs):
            cols = slice(j * LANES, (j + 1) * LANES)
            xp = act[:, pair * LANES:(pair + 1) * LANES]
            xpb = xp.astype(BF16)
            m_parts, acols, wcols = [], [], []
            for e in range(2):
                h = 2 * pair + e
                acol = jnp.broadcast_to(acs2[:, h:h + 1], (q, LANES))
                decay = jnp.exp2(acol - bdt_t[h:h + 1, :])
                m_parts.append(jnp.where(tril, scores * decay, 0.0).astype(BF16))
                acols.append(acol)
                wcols.append(jnp.broadcast_to(wst[:, h:h + 1], (q, LANES)))
            ea = jnp.exp2(jnp.where(lo_f, acols[0], acols[1]))
            y_ref[c, :, pair * LANES:(pair + 1) * LANES] = (
                jnp.dot(jnp.concatenate(m_parts, axis=1),
                        jnp.concatenate([xpb * lo_b, xpb * hi_b], axis=0),
                        preferred_element_type=F32)
                + ea * y_off[:, cols])
            xw_parts.append((xp * jnp.where(lo_f, wcols[0], wcols[1])).astype(BF16))
            cd_parts.append(ea[q - 1:q, :])
        upd = jnp.dot(b_gtb, jnp.concatenate(xw_parts, axis=1),
                      preferred_element_type=F32)
        for j, pair in enumerate(pairs):
            s_ref[pair] = s_g[j] * cd_parts[j] + upd[:, j * LANES:(j + 1) * LANES]
        yield

    for g in range(SSD_GROUPS):
        sl = slice(g * gwid, (g + 1) * gwid)
        y = y_ref[c, :, sl] + drep_ref[:, sl] * act[:, sl]
        y = y * _silu(z_ref[rows, sl].astype(F32))
        ms = jnp.mean(y * y, axis=-1, keepdims=True)
        ys_ref[rows, sl] = (y * lax.rsqrt(ms + NORM_EPS) * nw_ref[:, sl]).astype(ys_ref.dtype)


def _mixer_out_kernel(u_ref, gate_ref, z_ref, xbc_ref, dt_ref, x_ref,
                      band_ref, pband_ref, tril_ref, fix_ref, shift_ref, pshift_ref,
                      cw_ref, cb_ref, dtb_ref, alog_ref, drep_ref, snw_ref,
                      wp_ref, ws_ref, pnw_ref,
                      o_ref,
                      ptail_ref, ctail_ref, s_ref, y_ref, yp_ref, ys_ref, ypn_ref, ysn_ref, out_ref,
                      *, tiles_per_seq):
    tt = u_ref.shape[0]
    i = pl.program_id(0)
    seq_start = (i % tiles_per_seq) == 0

    @pl.when(i == 0)
    def _():
        yp_ref[...] = jnp.zeros_like(yp_ref)
        ys_ref[...] = jnp.zeros_like(ys_ref)

    @pl.when(seq_start)
    def _():
        ptail_ref[...] = jnp.zeros_like(ptail_ref)
        ctail_ref[...] = jnp.zeros_like(ctail_ref)
        s_ref[...] = jnp.zeros_like(s_ref)

    nchunks = tt // SSD_Q
    n_slabs = len(SLAB_SLOTS)
    slab = o_ref.shape[1] // n_slabs
    assert slab % (2 * LANES) == 0 and slab * n_slabs == o_ref.shape[1]
    emitted = []
    ssq = []

    def out_proj_slab(slot):
        if slot not in SLAB_SLOTS:
            return
        k = len(emitted)
        emitted.append(k)
        cols = slice(k * slab, (k + 1) * slab)
        piece = (jnp.dot(yp_ref[...], wp_ref[:, cols], preferred_element_type=F32)
                 + jnp.dot(ys_ref[...], ws_ref[:, cols], preferred_element_type=F32))
        out_ref[:, cols] = piece
        ssq.append(jnp.sum(piece * piece, axis=-1, keepdims=True))

    first = jnp.where(seq_start, 1.0, 0.0).astype(F32)
    live = [_ssd_chunk(c, z_ref, xbc_ref, dt_ref, shift_ref, pshift_ref, cw_ref, cb_ref, dtb_ref,
                       alog_ref, drep_ref, snw_ref, ctail_ref, s_ref, y_ref, ysn_ref,
                       out_proj_slab) for c in range(nchunks)]
    while live:
        live = [chunk for chunk in live if next(chunk, "done") != "done"]
    _pool_part(u_ref, gate_ref, band_ref, pband_ref, tril_ref, fix_ref, ptail_ref, ypn_ref, first,
               out_proj_slab)
    assert len(emitted) == n_slabs
    ptail_ref[...] = u_ref[tt - MAX_WINDOW:, :]
    ctail_ref[...] = xbc_ref[tt - BF16_ROWS:, :]

    out = out_ref[...]
    var = functools.reduce(lambda p, r: p + r, ssq) * (1.0 / out.shape[1])
    o_ref[...] = x_ref[...] + out * lax.rsqrt(var + NORM_EPS) * pnw_ref[...]

    yp_ref[...] = ypn_ref[...]
    ys_ref[...] = ysn_ref[...]


def _mixer_out(proj, dt_raw, x2d, conv_w16, conv_b, dt_bias, a_log, d_rep, ssd_nw,
               w_out, post_w, layer, batch, seq):
    m, d = x2d.shape
    ssd_w = ssd_nw.shape[-1]
    pw = ssd_w
    gw = pw // len(POOL_WINDOWS)
    conv_dim = conv_b.shape[-1]
    assert proj.shape[1] == 2 * pw + ssd_w + conv_dim and conv_dim * 2 == 3 * pw
    assert w_out.shape[1] == pw + ssd_w and MIX_TM % SSD_Q == 0
    nt = seq // MIX_TM
    ntiles = batch * nt
    npairs = ssd_w // LANES
    band, pband, tril, fix = _pool_constants(MIX_TM, gw)
    shift, pshift = _conv_constants(SSD_Q)
    row_blk = lambda col: (lambda i: (jnp.minimum(i, ntiles - 1), col))
    prev_blk = lambda i: (jnp.maximum(i - 1, 0), 0)
    resident = dict(pipeline_mode=pl.Buffered(1))
    return pl.pallas_call(
        functools.partial(_mixer_out_kernel, tiles_per_seq=nt),
        grid=(ntiles + 1,),
        in_specs=[
            pl.BlockSpec((MIX_TM, pw), row_blk(0)),
            pl.BlockSpec((MIX_TM, pw), row_blk(1)),
            pl.BlockSpec((MIX_TM, ssd_w), row_blk(2)),
            pl.BlockSpec((MIX_TM, conv_dim), row_blk(2)),
            pl.BlockSpec((MIX_TM, LANES), row_blk(0)),
            pl.BlockSpec((MIX_TM, d), prev_blk),
            _const_block(band), _const_block(pband), _const_block(tril), _const_block(fix),
            _const_block(shift), _const_block(pshift),
            _layer_block((CONV_WIDTH, BF16_ROWS, conv_dim), layer),
            _layer_block((1, conv_dim), layer),
            _layer_block((1, LANES), layer),
            _layer_block((1, LANES), layer),
            _layer_block((1, ssd_w), layer),
            _layer_block((1, ssd_w), layer),
            pl.BlockSpec((None, pw, d), lambda i: (layer, 0, 0), **resident),
            pl.BlockSpec((None, ssd_w, d), lambda i: (layer, 1, 0), **resident),
            _layer_block((1, d), layer),
        ],
        out_specs=pl.BlockSpec((MIX_TM, d), prev_blk),
        out_shape=jax.ShapeDtypeStruct((m, d), F32),
        scratch_shapes=[
            pltpu.VMEM((MAX_WINDOW, pw), BF16),
            pltpu.VMEM((BF16_ROWS, conv_dim), BF16),
            pltpu.VMEM((npairs, SSD_STATE, LANES), F32),
            pltpu.VMEM((MIX_TM // SSD_Q, SSD_Q, ssd_w), F32),
            pltpu.VMEM((MIX_TM, pw), BF16),
            pltpu.VMEM((MIX_TM, ssd_w), BF16),
            pltpu.VMEM((MIX_TM, pw), BF16),
            pltpu.VMEM((MIX_TM, ssd_w), BF16),
            pltpu.VMEM((MIX_TM, d), F32),
        ],
        compiler_params=pltpu.CompilerParams(
            dimension_semantics=("arbitrary",),
            vmem_limit_bytes=VMEM_LIMIT),
        name="mixer_out",
    )(proj, proj, proj, proj, dt_raw, x2d, band, pband, tril, fix, shift, pshift,
      conv_w16, conv_b, dt_bias, a_log, d_rep, ssd_nw, w_out, w_out, post_w)


def kernel(x, pre_norm_w, w_in, pool_mix_w, pool_scale, conv_w, conv_b, dt_bias, a_log, d_skip,
           ssd_norm_w, w_out, post_norm_w):
    batch, seq, d_model = x.shape
    depth = w_in.shape[0]
    pool_w = pool_scale.shape[1]
    ssd_w = ssd_norm_w.shape[1]
    heads = dt_bias.shape[1]
    conv_dim = conv_w.shape[2]
    n_main = w_in.shape[2] - heads
    assert ssd_w == heads * SSD_HEAD_DIM and heads <= LANES and pool_w == ssd_w
    assert n_main == 2 * pool_w + ssd_w + conv_dim and n_main % IN_TN == 0
    assert seq % MIX_TM == 0 and (batch * seq) % IN_TM == 0

    head_pad = ((0, 0), (0, LANES - heads))
    w_in_t = jnp.swapaxes(w_in, 1, 2)
    w_dt_t = jnp.pad(w_in_t[:, n_main:, :], ((0, 0), (0, LANES - heads), (0, 0))).astype(BF16)
    w_out_b = w_out.astype(BF16)
    conv_w16 = jnp.broadcast_to(conv_w.astype(BF16)[:, :, None, :],
                                (depth, CONV_WIDTH, BF16_ROWS, conv_dim))
    rows = lambda p: p.reshape(depth, 1, -1)
    pre_w, post_w = rows(pre_norm_w), rows(post_norm_w)
    conv_b3, ssd_nw = rows(conv_b), rows(ssd_norm_w)
    w_main_t = _prep_w_in(w_in_t, pool_mix_w, rows(pool_scale), n_main)
    dt_bias3, a_log3 = rows(jnp.pad(dt_bias, head_pad)), rows(jnp.pad(a_log, head_pad))
    d_rep = rows(jnp.repeat(d_skip, SSD_HEAD_DIM, axis=1))

    x2d = x.reshape(batch * seq, d_model)
    for layer in range(depth):
        proj, dt_raw = _in_proj(x2d, pre_w, w_main_t, w_dt_t, layer, n_main)
        x2d = _mixer_out(proj, dt_raw, x2d, conv_w16, conv_b3, dt_bias3, a_log3,
                         d_rep, ssd_nw, w_out_b, post_w, layer, batch, seq)
    return x2d.reshape(batch, seq, d_model)
```

```python
import functools

import jax
import jax.numpy as jnp
import numpy as np
from jax import lax
from jax.experimental import pallas as pl
from jax.experimental.pallas import tpu as pltpu

F32 = jnp.float32
BF16 = jnp.bfloat16

POOL_WINDOWS = (2, 4, 8, 16)
MAX_WINDOW = max(POOL_WINDOWS)
SSD_HEAD_DIM = 64
SSD_STATE = 128
SSD_GROUPS = 4
CONV_WIDTH = 4
NORM_EPS = 1e-6
LOG2E = 1.4426950408889634

LANES = 128
BF16_ROWS = 16
VMEM_LIMIT = 56 * 1024 * 1024

IN_TM = 1024
IN_TN = 2304
IN_CH = 256
MIX_TM = 256
SSD_Q = 128

SLAB_SLOTS = (("ssd", 0, 0), ("ssd", 0, 1), ("ssd", 0, 2), ("ssd", 0, 3),
              ("ssd", 1, 0), ("ssd", 1, 1), ("ssd", 1, 2), ("pool", 0))

NT_DIMS = (((1,), (1,)), ((), ()))


def _silu(v):
    hv = 0.5 * v
    return hv + hv * jnp.tanh(hv)


def _softplus(v):
    return jnp.maximum(v, 0.0) + jnp.log(1.0 + jnp.exp(-jnp.abs(v)))


def _layer_block(shape, layer, *tail):
    if tail:
        return pl.BlockSpec((None,) + shape, lambda *g: (layer,) + tuple(tail[0](*g)))
    return pl.BlockSpec((None,) + shape, lambda *g: (layer,) + (0,) * len(shape))


def _const_block(arr):
    return pl.BlockSpec(arr.shape, lambda *g: (0,) * arr.ndim)


def _in_proj_kernel(x_ref, nw_ref, w_ref, wdt_ref, o_ref, dt_ref, h_ref):
    j = pl.program_id(1)

    @pl.when(j == 0)
    def _():
        for r in range(IN_TM // IN_CH):
            rows = slice(r * IN_CH, (r + 1) * IN_CH)
            x = x_ref[rows, :]
            var = jnp.mean(x * x, axis=-1, keepdims=True)
            h = (x * lax.rsqrt(var + NORM_EPS) * nw_ref[...]).astype(BF16)
            h_ref[rows, :] = h
            dt_ref[rows, :] = lax.dot_general(h, wdt_ref[...], NT_DIMS, preferred_element_type=F32)
            o_ref[rows, :] = lax.dot_general(h, w_ref[...], NT_DIMS,
                                             preferred_element_type=F32).astype(o_ref.dtype)

    @pl.when(j != 0)
    def _():
        o_ref[...] = lax.dot_general(h_ref[...], w_ref[...], NT_DIMS,
                                     preferred_element_type=F32).astype(o_ref.dtype)


def _in_proj(x2d, norm_w, w_in_t, w_dt_t, layer, n_main):
    m, d = x2d.shape
    return pl.pallas_call(
        _in_proj_kernel,
        grid=(m // IN_TM, n_main // IN_TN),
        in_specs=[
            pl.BlockSpec((IN_TM, d), lambda i, j: (i, 0)),
            _layer_block((1, d), layer),
            _layer_block((IN_TN, d), layer, lambda i, j: (j, 0)),
            _layer_block((LANES, d), layer),
        ],
        out_specs=[
            pl.BlockSpec((IN_TM, IN_TN), lambda i, j: (i, j)),
            pl.BlockSpec((IN_TM, LANES), lambda i, j: (i, 0)),
        ],
        out_shape=[
            jax.ShapeDtypeStruct((m, n_main), BF16),
            jax.ShapeDtypeStruct((m, LANES), F32),
        ],
        scratch_shapes=[pltpu.VMEM((IN_TM, d), BF16)],
        compiler_params=pltpu.CompilerParams(
            dimension_semantics=("arbitrary", "arbitrary"),
            vmem_limit_bytes=VMEM_LIMIT),
        name="in_proj",
    )(x2d, norm_w, w_in_t, w_dt_t)


def _prep_kernel(w_ref, mix_ref, scale_ref, o_ref, *, fold_blocks):
    blk = pl.program_id(1)

    @pl.when(blk < fold_blocks)
    def _():
        wm = (mix_ref[...] * scale_ref[...]).astype(BF16)
        o_ref[...] = lax.dot_general(wm, w_ref[...].astype(BF16), (((0,), (0,)), ((), ())),
                                     preferred_element_type=F32).astype(o_ref.dtype)

    @pl.when(blk >= fold_blocks)
    def _():
        o_ref[...] = w_ref[...].astype(o_ref.dtype)


def _prep_w_in(w_in_t, mix_w, scale, n_main):
    depth, _, d = w_in_t.shape
    _, ngroups, gw, _ = mix_w.shape
    assert n_main % gw == 0
    group = lambda blk: jnp.minimum(blk, ngroups - 1)
    rows_blk = pl.BlockSpec((None, gw, d), lambda l, b: (l, b, 0))
    return pl.pallas_call(
        functools.partial(_prep_kernel, fold_blocks=ngroups),
        grid=(depth, n_main // gw),
        in_specs=[
            rows_blk,
            pl.BlockSpec((None, None, gw, gw), lambda l, b: (l, group(b), 0, 0)),
            pl.BlockSpec((None, 1, gw), lambda l, b: (l, 0, group(b))),
        ],
        out_specs=rows_blk,
        out_shape=jax.ShapeDtypeStruct((depth, n_main, d), BF16),
        compiler_params=pltpu.CompilerParams(
            dimension_semantics=("arbitrary", "arbitrary"),
            vmem_limit_bytes=VMEM_LIMIT),
        name="prep_w_in",
    )(w_in_t, mix_w, scale)


def _pool_constants(tt, gw):
    i = np.arange(tt)[:, None]
    j = np.arange(tt)[None, :]
    pi = np.arange(MAX_WINDOW)[:, None]
    pj = np.arange(MAX_WINDOW)[None, :]
    band, pband, fix = [], [], []
    for w in POOL_WINDOWS:
        inside = (j <= i) & (j > i - w)
        band.append(inside / w - (i == j))
        pband.append((pj > pi + (MAX_WINDOW - w)) / w)
        fix.append(np.repeat(1.0 / np.minimum(pi + 1, w) - 1.0 / w, gw, axis=1))
    return (jnp.asarray(np.stack(band), dtype=BF16), jnp.asarray(np.stack(pband), dtype=BF16),
            jnp.asarray(pj <= pi, dtype=BF16), jnp.asarray(np.concatenate(fix, axis=1), dtype=F32))


def _conv_constants(q):
    t = np.arange(q)[:, None]
    s = np.arange(q)[None, :]
    pt = np.arange(BF16_ROWS)[:, None]
    pj = np.arange(BF16_ROWS)[None, :]
    shift, pshift = [], []
    for k in range(CONV_WIDTH):
        d = CONV_WIDTH - 1 - k
        shift.append(s == t - d)
        if d:
            pshift.append((pj == BF16_ROWS - d + pt) & (pt < d))
    return (jnp.asarray(np.concatenate(shift, axis=1), dtype=BF16),
            jnp.asarray(np.concatenate(pshift, axis=1), dtype=BF16))


def _pool_part(u_ref, gate_ref, band_ref, pband_ref, tril_ref, fix_ref, ptail_ref, yp_ref, first,
               side_work):
    gw = u_ref.shape[1] // len(POOL_WINDOWS)
    head_rows = slice(0, MAX_WINDOW)
    for g in range(len(POOL_WINDOWS)):
        side_work(("pool", g))
        sl = slice(g * gw, (g + 1) * gw)
        pooled = jnp.dot(band_ref[g], u_ref[:, sl], preferred_element_type=F32)
        head = jnp.dot(pband_ref[g], ptail_ref[:, sl], preferred_element_type=F32)
        start = jnp.dot(tril_ref[...], u_ref[head_rows, sl], preferred_element_type=F32)
        head = head + start * (first * fix_ref[:, sl])
        pooled = jnp.concatenate([pooled[:MAX_WINDOW] + head, pooled[MAX_WINDOW:]], axis=0)
        gate = gate_ref[:, sl].astype(F32)
        yp_ref[:, sl] = (pooled * _silu(gate)).astype(yp_ref.dtype)


def _ssd_chunk(c, z_ref, xbc_ref, dt_ref, shift_ref, pshift_ref, cw_ref, cb_ref, dtb_ref, alog_ref,
               drep_ref, nw_ref, ctail_ref, s_ref, y_ref, ys_ref, side_work):
    q = SSD_Q
    ssd_w = z_ref.shape[1]
    gn = SSD_GROUPS * SSD_STATE
    gwid = ssd_w // SSD_GROUPS
    pairs_per_group = gwid // LANES
    rows = slice(c * q, (c + 1) * q)

    def taps(frames, ntaps):
        nrow = frames.shape[0] // BF16_ROWS
        return jnp.concatenate(
            [frames[r * BF16_ROWS:(r + 1) * BF16_ROWS, :] * cw_ref[k]
             for k in range(ntaps) for r in range(nrow)], axis=0)

    before = ctail_ref[...] if c == 0 else xbc_ref[c * q - BF16_ROWS:c * q, :]
    acc = jnp.dot(shift_ref[...], taps(xbc_ref[rows, :], CONV_WIDTH), preferred_element_type=F32)
    head = jnp.dot(pshift_ref[...], taps(before, CONV_WIDTH - 1), preferred_element_type=F32)
    acc = jnp.concatenate([acc[:BF16_ROWS] + head, acc[BF16_ROWS:]], axis=0) + cb_ref[...]
    act = _silu(acc)

    dt = _softplus(dt_ref[rows, :] + dtb_ref[...])
    adt = dt * (-jnp.exp(alog_ref[...]))
    tril = (lax.broadcasted_iota(jnp.int32, (q, q), 0)
            >= lax.broadcasted_iota(jnp.int32, (q, q), 1))
    tri = jnp.where(tril, 1.0, 0.0).astype(BF16)
    a1 = adt.astype(BF16)
    r1 = adt - a1.astype(F32)
    a2 = r1.astype(BF16)
    a3 = (r1 - a2.astype(F32)).astype(BF16)
    acs2 = LOG2E * (jnp.dot(tri, a1, preferred_element_type=F32)
                    + jnp.dot(tri, a2, preferred_element_type=F32)
                    + jnp.dot(tri, a3, preferred_element_type=F32))
    bdt = acs2 - LOG2E * jnp.log(dt)
    wst = jnp.exp2(acs2[q - 1:q, :] - bdt)
    bdt_t = bdt.T

    yield
    lo_f =lax.broadcasted_iota(jnp.int32, (q, LANES), 1) < SSD_HEAD_DIM
    lo_b = jnp.where(lo_f, 1.0, 0.0).astype(BF16)
    hi_b = jnp.where(lo_f, 0.0, 1.0).astype(BF16)

    for g in range(SSD_GROUPS):
        side_work(("ssd", c, g))
        b_g = act[:, ssd_w + g * SSD_STATE: ssd_w + (g + 1) * SSD_STATE]
        c_g = act[:, ssd_w + gn + g * SSD_STATE: ssd_w + gn + (g + 1) * SSD_STATE]
        c_gb = c_g.astype(BF16)
        scores = lax.dot_general(c_gb, b_g.astype(BF16), NT_DIMS,
                                 preferred_element_type=F32)
        scores_b = scores.astype(BF16)
        b_gtb = b_g.T.astype(BF16)
        pairs = range(g * pairs_per_group, (g + 1) * pairs_per_group)
        s_g = [s_ref[pair] for pair in pairs]
        y_off = jnp.dot(c_gb, jnp.concatenate([sp.astype(BF16) for sp in s_g], axis=1),
                        preferred_element_type=F32)
        xw_parts, cd_parts = [], []
        for j, pair in enumerate(pairs):
            cols = slice(j * LANES, (j + 1) * LANES)
            xp = act[:, pair * LANES:(pair + 1) * LANES]
            xpb = xp.astype(BF16)
            m_parts, acols, wcols = [], [], []
            for e in range(2):
                h = 2 * pair + e
                acol = jnp.broadcast_to(acs2[:, h:h + 1], (q, LANES))
                decay = jnp.exp2(acol - bdt_t[h:h + 1, :])
                m_parts.append(jnp.where(tril, scores_b * decay.astype(BF16), 0.0))
                acols.append(acol)
                wcols.append(jnp.broadcast_to(wst[:, h:h + 1], (q, LANES)))
            ea = jnp.exp2(jnp.where(lo_f, acols[0], acols[1]))
            y_ref[c, :, pair * LANES:(pair + 1) * LANES] = (
                jnp.dot(jnp.concatenate(m_parts, axis=1),
                        jnp.concatenate([xpb * lo_b, xpb * hi_b], axis=0),
                        preferred_element_type=F32)
                + ea * y_off[:, cols])
            xw_parts.append((xp * jnp.where(lo_f, wcols[0], wcols[1])).astype(BF16))
            cd_parts.append(ea[q - 1:q, :])
        upd = jnp.dot(b_gtb, jnp.concatenate(xw_parts, axis=1),
                      preferred_element_type=F32)
        for j, pair in enumerate(pairs):
            s_ref[pair] = s_g[j] * cd_parts[j] + upd[:, j * LANES:(j + 1) * LANES]
        yield

    for g in range(SSD_GROUPS):
        sl = slice(g * gwid, (g + 1) * gwid)
        y = y_ref[c, :, sl] + drep_ref[:, sl] * act[:, sl]
        y = y * _silu(z_ref[rows, sl].astype(F32))
        ms = jnp.mean(y * y, axis=-1, keepdims=True)
        ys_ref[rows, sl] = (y * lax.rsqrt(ms + NORM_EPS) * nw_ref[:, sl]).astype(ys_ref.dtype)


def _mixer_out_kernel(u_ref, gate_ref, z_ref, xbc_ref, dt_ref, x_ref,
                      band_ref, pband_ref, tril_ref, fix_ref, shift_ref, pshift_ref,
                      cw_ref, cb_ref, dtb_ref, alog_ref, drep_ref, snw_ref,
                      wp_ref, ws_ref, pnw_ref,
                      o_ref,
                      ptail_ref, ctail_ref, s_ref, y_ref, yp_ref, ys_ref, ypn_ref, ysn_ref, out_ref,
                      *, tiles_per_seq):
    tt = u_ref.shape[0]
    i = pl.program_id(0)
    seq_start = (i % tiles_per_seq) == 0

    @pl.when(i == 0)
    def _():
        yp_ref[...] = jnp.zeros_like(yp_ref)
        ys_ref[...] = jnp.zeros_like(ys_ref)

    @pl.when(seq_start)
    def _():
        ptail_ref[...] = jnp.zeros_like(ptail_ref)
        ctail_ref[...] = jnp.zeros_like(ctail_ref)
        s_ref[...] = jnp.zeros_like(s_ref)

    nchunks = tt // SSD_Q
    n_slabs = len(SLAB_SLOTS)
    slab = o_ref.shape[1] // n_slabs
    assert slab % (2 * LANES) == 0 and slab * n_slabs == o_ref.shape[1]
    emitted = []
    ssq = []

    def out_proj_slab(slot):
        if slot not in SLAB_SLOTS:
            return
        k = len(emitted)
        emitted.append(k)
        cols = slice(k * slab, (k + 1) * slab)
        piece = (jnp.dot(yp_ref[...], wp_ref[:, cols], preferred_element_type=F32)
                 + jnp.dot(ys_ref[...], ws_ref[:, cols], preferred_element_type=F32))
        out_ref[:, cols] = piece
        ssq.append(jnp.sum(piece * piece, axis=-1, keepdims=True))

    first = jnp.where(seq_start, 1.0, 0.0).astype(F32)
    live = [_ssd_chunk(c, z_ref, xbc_ref, dt_ref, shift_ref, pshift_ref, cw_ref, cb_ref, dtb_ref,
                       alog_ref, drep_ref, snw_ref, ctail_ref, s_ref, y_ref, ysn_ref,
                       out_proj_slab) for c in range(nchunks)]
    while live:
        live = [chunk for chunk in live if next(chunk, "done") != "done"]
    _pool_part(u_ref, gate_ref, band_ref, pband_ref, tril_ref, fix_ref, ptail_ref, ypn_ref, first,
               out_proj_slab)
    assert len(emitted) == n_slabs
    ptail_ref[...] = u_ref[tt - MAX_WINDOW:, :]
    ctail_ref[...] = xbc_ref[tt - BF16_ROWS:, :]

    out = out_ref[...]
    var = functools.reduce(lambda p, r: p + r, ssq) * (1.0 / out.shape[1])
    o_ref[...] = x_ref[...] + out * lax.rsqrt(var + NORM_EPS) * pnw_ref[...]

    yp_ref[...] = ypn_ref[...]
    ys_ref[...] = ysn_ref[...]


def _mixer_out(proj, dt_raw, x2d, conv_w16, conv_b, dt_bias, a_log, d_rep, ssd_nw,
               w_out, post_w, layer, batch, seq):
    m, d = x2d.shape
    ssd_w = ssd_nw.shape[-1]
    pw = ssd_w
    gw = pw // len(POOL_WINDOWS)
    conv_dim = conv_b.shape[-1]
    assert proj.shape[1] == 2 * pw + ssd_w + conv_dim and conv_dim * 2 == 3 * pw
    assert w_out.shape[1] == pw + ssd_w and MIX_TM % SSD_Q == 0
    nt = seq // MIX_TM
    ntiles = batch * nt
    npairs = ssd_w // LANES
    band, pband, tril, fix = _pool_constants(MIX_TM, gw)
    shift, pshift = _conv_constants(SSD_Q)
    row_blk = lambda col: (lambda i: (jnp.minimum(i, ntiles - 1), col))
    prev_blk = lambda i: (jnp.maximum(i - 1, 0), 0)
    resident = dict(pipeline_mode=pl.Buffered(1))
    return pl.pallas_call(
        functools.partial(_mixer_out_kernel, tiles_per_seq=nt),
        grid=(ntiles + 1,),
        in_specs=[
            pl.BlockSpec((MIX_TM, pw), row_blk(0)),
            pl.BlockSpec((MIX_TM, pw), row_blk(1)),
            pl.BlockSpec((MIX_TM, ssd_w), row_blk(2)),
            pl.BlockSpec((MIX_TM, conv_dim), row_blk(2)),
            pl.BlockSpec((MIX_TM, LANES), row_blk(0)),
            pl.BlockSpec((MIX_TM, d), prev_blk),
            _const_block(band), _const_block(pband), _const_block(tril), _const_block(fix),
            _const_block(shift), _const_block(pshift),
            _layer_block((CONV_WIDTH, BF16_ROWS, conv_dim), layer),
            _layer_block((1, conv_dim), layer),
            _layer_block((1, LANES), layer),
            _layer_block((1, LANES), layer),
            _layer_block((1, ssd_w), layer),
            _layer_block((1, ssd_w), layer),
            pl.BlockSpec((None, pw, d), lambda i: (layer, 0, 0), **resident),
            pl.BlockSpec((None, ssd_w, d), lambda i: (layer, 1, 0), **resident),
            _layer_block((1, d), layer),
        ],
        out_specs=pl.BlockSpec((MIX_TM, d), prev_blk),
        out_shape=jax.ShapeDtypeStruct((m, d), F32),
        scratch_shapes=[
            pltpu.VMEM((MAX_WINDOW, pw), BF16),
            pltpu.VMEM((BF16_ROWS, conv_dim), BF16),
            pltpu.VMEM((npairs, SSD_STATE, LANES), F32),
            pltpu.VMEM((MIX_TM // SSD_Q, SSD_Q, ssd_w), F32),
            pltpu.VMEM((MIX_TM, pw), BF16),
            pltpu.VMEM((MIX_TM, ssd_w), BF16),
            pltpu.VMEM((MIX_TM, pw), BF16),
            pltpu.VMEM((MIX_TM, ssd_w), BF16),
            pltpu.VMEM((MIX_TM, d), F32),
        ],
        compiler_params=pltpu.CompilerParams(
            dimension_semantics=("arbitrary",),
            vmem_limit_bytes=VMEM_LIMIT),
        name="mixer_out",
    )(proj, proj, proj, proj, dt_raw, x2d, band, pband, tril, fix, shift, pshift,
      conv_w16, conv_b, dt_bias, a_log, d_rep, ssd_nw, w_out, w_out, post_w)


def kernel(x, pre_norm_w, w_in, pool_mix_w, pool_scale, conv_w, conv_b, dt_bias, a_log, d_skip,
           ssd_norm_w, w_out, post_norm_w):
    batch, seq, d_model = x.shape
    depth = w_in.shape[0]
    pool_w = pool_scale.shape[1]
    ssd_w = ssd_norm_w.shape[1]
    heads = dt_bias.shape[1]
    conv_dim = conv_w.shape[2]
    n_main = w_in.shape[2] - heads
    assert ssd_w == heads * SSD_HEAD_DIM and heads <= LANES and pool_w == ssd_w
    assert n_main == 2 * pool_w + ssd_w + conv_dim and n_main % IN_TN == 0
    assert seq % MIX_TM == 0 and (batch * seq) % IN_TM == 0

    head_pad = ((0, 0), (0, LANES - heads))
    w_in_t = jnp.swapaxes(w_in, 1, 2)
    w_dt_t = jnp.pad(w_in_t[:, n_main:, :], ((0, 0), (0, LANES - heads), (0, 0))).astype(BF16)
    w_out_b = w_out.astype(BF16)
    conv_w16 = jnp.broadcast_to(conv_w.astype(BF16)[:, :, None, :],
                                (depth, CONV_WIDTH, BF16_ROWS, conv_dim))
    rows = lambda p: p.reshape(depth, 1, -1)
    pre_w, post_w = rows(pre_norm_w), rows(post_norm_w)
    conv_b3, ssd_nw = rows(conv_b), rows(ssd_norm_w)
    w_main_t = _prep_w_in(w_in_t, pool_mix_w, rows(pool_scale), n_main)
    dt_bias3, a_log3 = rows(jnp.pad(dt_bias, head_pad)), rows(jnp.pad(a_log, head_pad))
    d_rep = rows(jnp.repeat(d_skip, SSD_HEAD_DIM, axis=1))

    x2d = x.reshape(batch * seq, d_model)
    for layer in range(depth):
        proj, dt_raw = _in_proj(x2d, pre_w, w_main_t, w_dt_t, layer, n_main)
        x2d = _mixer_out(proj, dt_raw, x2d, conv_w16, conv_b3, dt_bias3, a_log3,
                         d_rep, ssd_nw, w_out_b, post_w, layer, batch, seq)
    return x2d.reshape(batch, seq, d_model)
```
